```python
import math
import jax, jax.numpy as jnp
from jax import lax
import numpy as np

D_MODEL = 2048
BATCH = 8
SEQ = 2048
DEPTH = 2
DEC_BATCH = 32
DEC_SEQ = 1
PAST_LEN = 8192
PAGE_SIZE = 128

N_EVEN = (DEPTH + 1) // 2
N_ODD = DEPTH // 2

A_HEAD_DIM = 128
A_HEADS = D_MODEL // (2 * A_HEAD_DIM)
A_WIDTH = A_HEADS * A_HEAD_DIM
DILATION_PATTERNS = ((128, 1), (512, 4), (2048, 16))
WINDOW_MAX = 2048
QBLK = 128

REL_BUCKETS = 32
REL_MAX_DIST = 2048

CHUNK = 128
B_GROUP_CH = 128
B_WIDTH = D_MODEL // 2
B_GROUPS = B_WIDTH // B_GROUP_CH

AB_COLS = 4 * A_WIDTH + 3 * B_WIDTH
AB_SPLITS = (A_WIDTH, 2 * A_WIDTH, 3 * A_WIDTH, 4 * A_WIDTH, 4 * A_WIDTH + B_WIDTH, 4 * A_WIDTH + 2 * B_WIDTH)

C_HEAD_DIM = 64
C_HEADS = D_MODEL // C_HEAD_DIM
C_DECAY_LORA = 96
C_ICLR_LORA = 96

PLE_DIM = 256
RMS_EPS = 1e-6
LN_EPS = 1e-5
GN_EPS = 64e-5
NEG_INF = -1e30

kernel_name = 'hybrid_dilated_gmlp_rwkv7_step'


def rms_norm(x, g):
    xf = x.astype(jnp.float32)
    y = xf * lax.rsqrt(jnp.mean(xf * xf, axis=-1, keepdims=True) + RMS_EPS)
    return (y * g.astype(jnp.float32)).astype(x.dtype)


def layer_norm(x, g, b):
    xf = x.astype(jnp.float32)
    mu = jnp.mean(xf, axis=-1, keepdims=True)
    var = jnp.mean(jnp.square(xf - mu), axis=-1, keepdims=True)
    return ((xf - mu) * lax.rsqrt(var + LN_EPS) * g.astype(jnp.float32) + b.astype(jnp.float32)).astype(x.dtype)


def t5_bucket(dist):
    n_exact = REL_BUCKETS // 2
    d = jnp.maximum(dist, 1).astype(jnp.float32)
    log_b = n_exact + (jnp.log(d / n_exact) / math.log(REL_MAX_DIST / n_exact) * (REL_BUCKETS - n_exact)).astype(jnp.int32)
    return jnp.where(dist < n_exact, dist, jnp.minimum(log_b, REL_BUCKETS - 1))


def dilated_attn_prompt(q, k, v, rel_bias, window, dil):
    Bn, S, H, E = q.shape
    n_steps = window // dil
    span = dil * QBLK
    Sp = -(-S // span) * span
    nb = Sp // span

    def blocks(t):
        t = jnp.pad(t, ((0, 0), (0, Sp - S), (0, 0), (0, 0)))
        return t.reshape(Bn, nb, QBLK, dil, H, E)

    def with_prev(t):
        prev = jnp.pad(t, ((0, 0), (1, 0), (0, 0), (0, 0), (0, 0), (0, 0)))[:, :-1]
        return jnp.concatenate([prev, t], axis=2)

    qb = blocks(q)
    kc = with_prev(blocks(k))
    vc = with_prev(blocks(v))
    s = jnp.einsum('bnirhe,bnjrhe->bnrhij', qb, kc, preferred_element_type=jnp.float32) * (E ** -0.5)
    i = jnp.arange(QBLK)[:, None]
    j = jnp.arange(2 * QBLK)[None, :]
    steps = QBLK + i - j
    band = (steps >= 0) & (steps <= n_steps)
    has_prev = (jnp.arange(nb)[:, None, None] > 0) | (j >= QBLK)[None]
    valid = band[None] & has_prev
    bias = jnp.moveaxis(rel_bias[t5_bucket(jnp.clip(steps, 0) * dil)], -1, 0).astype(jnp.float32)
    s = jnp.where(valid[None, :, None, None], s + bias, NEG_INF)
    lse = jax.nn.logsumexp(s, axis=-1)
    p = jnp.exp(s - lse[..., None]).astype(v.dtype)
    o = jnp.einsum('bnrhij,bnjrhe->bnirhe', p, vc, preferred_element_type=jnp.float32)
    o = o.reshape(Bn, Sp, H, E)[:, :S]
    lse = jnp.transpose(lse, (0, 1, 4, 2, 3)).reshape(Bn, Sp, H)[:, :S]
    return o, lse


def dilated_attn_sample(q, k_all, v_all, rel_bias, window, dil):
    T, E = q.shape[1], q.shape[-1]
    L = k_all.shape[1] - T
    back = jnp.arange(window // dil + 1)
    idx = L + jnp.arange(T)[:, None] - back[None, :] * dil
    valid = idx >= 0
    idx = jnp.maximum(idx, 0)
    kg = k_all[:, idx]
    vg = v_all[:, idx]
    s = jnp.einsum('bthe,btshe->bths', q, kg, preferred_element_type=jnp.float32) * (E ** -0.5)
    bias = rel_bias[t5_bucket(back * dil)].T.astype(jnp.float32)
    s = jnp.where(valid[None, :, None, :], s + bias, NEG_INF)
    lse = jax.nn.logsumexp(s, axis=-1)
    p = jnp.exp(s - lse[..., None]).astype(vg.dtype)
    o = jnp.einsum('bths,btshe->bthe', p, vg, preferred_element_type=jnp.float32)
    return o, lse


def mix_by_denominator(results):
    outs, lses = zip(*results)
    wts = jax.nn.softmax(jnp.stack(lses), axis=0)
    return jnp.einsum('pbth,pbthe->bthe', wts, jnp.stack(outs))


def chunk_spatial_mix(v, w_s, b_s):
    Bn, T, G, C = v.shape
    Tp = -(-T // CHUNK) * CHUNK
    vp = jnp.pad(v, ((0, 0), (0, Tp - T), (0, 0), (0, 0))).reshape(Bn, Tp // CHUNK, CHUNK, G, C)
    w = w_s * jnp.tril(jnp.ones((CHUNK, CHUNK), w_s.dtype))
    out = jnp.einsum('gij,bnjgc->bnigc', w, vp) + b_s.T[None, None, :, :, None]
    return out.reshape(Bn, Tp, G, C)[:, :T]


def ab_layer(xn, attend_a, w_in, w_out, w_s, b_s, ln_g, ln_b):
    Bn, T, _ = xn.shape
    z = xn @ w_in
    q, k, v, g_a, u_b, v_b, g_b = jnp.split(z, AB_SPLITS, axis=-1)
    heads = lambda t: t.reshape(Bn, T, A_HEADS, A_HEAD_DIM)
    k_h, v_h = heads(k), heads(v)
    o_a = attend_a(heads(q), k_h, v_h).reshape(Bn, T, A_WIDTH).astype(xn.dtype)
    vn = layer_norm(jax.nn.gelu(v_b), ln_g, ln_b)
    s_b = chunk_spatial_mix(vn.reshape(Bn, T, B_GROUPS, B_GROUP_CH), w_s, b_s).reshape(Bn, T, B_WIDTH)
    o_b = jax.nn.gelu(u_b) * s_b.astype(xn.dtype)
    y = jnp.concatenate([o_a * jax.nn.silu(g_a), o_b * jax.nn.silu(g_b)], axis=-1) @ w_out
    return y, k_h, v_h, vn


def rwkv7_time_mix(xn, shift_prev, wkv_prev, mu, w_r, w_k, w_v, w_g, w_o, w0, w1, w2, a0, a1, a2, k_k, k_a, r_k, gn_g, gn_b):
    Bn, T, D = xn.shape
    H, N = C_HEADS, C_HEAD_DIM
    f32 = jnp.float32
    x_prev = jnp.concatenate([shift_prev[:, None, :].astype(xn.dtype), xn[:, :-1]], axis=1)
    xx = x_prev - xn
    xr, xw, xk, xv, xa, xg = (xn + xx * mu[m] for m in range(6))
    r = (xr @ w_r).astype(f32)
    k = (xk @ w_k).astype(f32)
    v = (xv @ w_v).astype(f32)
    g = jax.nn.silu(xg @ w_g)
    w_log = -jax.nn.softplus(-(w0 + jnp.tanh(xw @ w1) @ w2).astype(f32)) - 0.5
    decay = jnp.exp(-jnp.exp(w_log))
    a = jax.nn.sigmoid((a0 + (xa @ a1) @ a2).astype(f32))
    heads = lambda t: t.reshape(Bn, T, H, N)
    kk = heads(k * k_k.astype(f32))
    kk = kk / jnp.maximum(jnp.sqrt(jnp.sum(kk * kk, axis=-1, keepdims=True)), 1e-12)
    k = k * (1.0 + (a - 1.0) * k_a.astype(f32))
    r_h, w_h, k_h, v_h, a_h = heads(r), heads(decay), heads(k), heads(v), heads(a)

    def step(S, inp):
        r_t, w_t, k_t, v_t, kk_t, a_t = inp
        sa = jnp.einsum('bhij,bhj->bhi', S, -kk_t)
        S = S * w_t[:, :, None, :] + sa[..., :, None] * (kk_t * a_t)[..., None, :] + v_t[..., :, None] * k_t[..., None, :]
        return S, jnp.einsum('bhij,bhj->bhi', S, r_t)

    tm = lambda t: jnp.moveaxis(t, 1, 0)
    S_final, y = lax.scan(step, wkv_prev.astype(f32), (tm(r_h), tm(w_h), tm(k_h), tm(v_h), tm(kk), tm(a_h)))
    y = jnp.moveaxis(y, 0, 1)
    mean = jnp.mean(y, axis=-1, keepdims=True)
    var = jnp.mean(jnp.square(y - mean), axis=-1, keepdims=True)
    y = (y - mean) * lax.rsqrt(var + GN_EPS) * gn_g.reshape(H, N).astype(f32) + gn_b.reshape(H, N).astype(f32)
    y = y + jnp.sum(r_h * k_h * r_k.astype(f32), axis=-1, keepdims=True) * v_h
    out = (y.reshape(Bn, T, D).astype(xn.dtype) * g) @ w_o
    return out, S_final, xn[:, -1]


def per_layer_input(h, p, w_proj, w_gate):
    return jax.nn.sigmoid(h @ w_gate) * (p @ w_proj)


def setup_inputs(seed: int = 0) -> dict:
    key = jax.random.key(seed)
    ks = iter(jax.random.split(key, 48))
    nrm = lambda shape, scale: jax.random.normal(next(ks), shape, jnp.float32) * scale
    D = D_MODEL
    win_s = min(WINDOW_MAX, PAST_LEN)
    return {
        'x_prompt': nrm((BATCH, SEQ, D), 1.0),
        'x_sample': nrm((DEC_BATCH, DEC_SEQ, D), 1.0),
        'cache_a_k': nrm((N_EVEN, DEC_BATCH, win_s, A_HEADS, A_HEAD_DIM), 1.0),
        'cache_a_v': nrm((N_EVEN, DEC_BATCH, win_s, A_HEADS, A_HEAD_DIM), 1.0),
        'state_c_wkv': nrm((N_ODD, DEC_BATCH, C_HEADS, C_HEAD_DIM, C_HEAD_DIM), 0.1),
        'state_c_shift': nrm((N_ODD, DEC_BATCH, D), 1.0),
        'p_prompt': nrm((DEPTH, BATCH, SEQ, PLE_DIM), 1.0),
        'p_sample': nrm((DEPTH, DEC_BATCH, DEC_SEQ, PLE_DIM), 1.0),
        'norm_g': 1.0 + nrm((DEPTH, D), 0.02),
        'final_norm_g': 1.0 + nrm((D,), 0.02),
        'rel_bias': nrm((REL_BUCKETS, A_HEADS), 0.5),
        'ab_w_in': nrm((N_EVEN, D, AB_COLS), D ** -0.5),
        'ab_w_out': nrm((N_EVEN, A_WIDTH + B_WIDTH, D), (A_WIDTH + B_WIDTH) ** -0.5),
        'b_w_s': nrm((N_EVEN, B_GROUPS, CHUNK, CHUNK), CHUNK ** -0.5),
        'b_b_s': 1.0 + nrm((N_EVEN, B_GROUPS, CHUNK), 0.1),
        'b_ln_g': 1.0 + nrm((N_EVEN, B_WIDTH), 0.02),
        'b_ln_b': nrm((N_EVEN, B_WIDTH), 0.02),
        'c_mu': jax.random.uniform(next(ks), (N_ODD, 6, D), jnp.float32),
        'c_w_r': nrm((N_ODD, D, D), D ** -0.5),
        'c_w_k': nrm((N_ODD, D, D), D ** -0.5),
        'c_w_v': nrm((N_ODD, D, D), D ** -0.5),
        'c_w_g': nrm((N_ODD, D, D), D ** -0.5),
        'c_w_o': nrm((N_ODD, D, D), D ** -0.5),
        'c_w0': -2.0 + nrm((N_ODD, D), 0.5),
        'c_w1': nrm((N_ODD, D, C_DECAY_LORA), D ** -0.5),
        'c_w2': nrm((N_ODD, C_DECAY_LORA, D), 0.5 * C_DECAY_LORA ** -0.5),
        'c_a0': nrm((N_ODD, D), 0.1),
        'c_a1': nrm((N_ODD, D, C_ICLR_LORA), D ** -0.5),
        'c_a2': nrm((N_ODD, C_ICLR_LORA, D), 0.5 * C_ICLR_LORA ** -0.5),
        'c_k_k': 0.85 + nrm((N_ODD, D), 0.05),
        'c_k_a': 1.0 + nrm((N_ODD, D), 0.05),
        'c_r_k': nrm((N_ODD, C_HEADS, C_HEAD_DIM), 0.1),
        'c_gn_g': 1.0 + nrm((N_ODD, D), 0.02),
        'c_gn_b': nrm((N_ODD, D), 0.02),
        'ple_w_proj': nrm((DEPTH, PLE_DIM, D), 0.5 * PLE_DIM ** -0.5),
        'ple_w_gate': nrm((DEPTH, D, D), D ** -0.5),
    }


def reference(x_prompt, x_sample, cache_a_k, cache_a_v, state_c_wkv, state_c_shift, p_prompt, p_sample,
              norm_g, final_norm_g, rel_bias, ab_w_in, ab_w_out, b_w_s, b_b_s, b_ln_g, b_ln_b,
              c_mu, c_w_r, c_w_k, c_w_v, c_w_g, c_w_o, c_w0, c_w1, c_w2, c_a0, c_a1, c_a2,
              c_k_k, c_k_a, c_r_k, c_gn_g, c_gn_b, ple_w_proj, ple_w_gate):
    hp, hs = x_prompt, x_sample
    Bp, Sp_len = x_prompt.shape[0], x_prompt.shape[1]
    win_p = min(WINDOW_MAX, Sp_len)
    a_k_p, a_v_p, a_k_s, a_v_s, b_v_s = [], [], [], [], []
    c_S_p, c_x_p, c_S_s, c_x_s = [], [], [], []

    def attend_prompt(q, k, v):
        return mix_by_denominator([dilated_attn_prompt(q, k, v, rel_bias, w, d) for w, d in DILATION_PATTERNS])

    for i in range(DEPTH):
        j = i // 2
        xp = rms_norm(hp, norm_g[i])
        xs = rms_norm(hs, norm_g[i])
        if i % 2 == 0:
            ab_w = (ab_w_in[j], ab_w_out[j], b_w_s[j], b_b_s[j], b_ln_g[j], b_ln_b[j])
            mp, k_p, v_p, _ = ab_layer(xp, attend_prompt, *ab_w)
            ck, cv = cache_a_k[j], cache_a_v[j]

            def attend_sample(q, k, v, ck=ck, cv=cv):
                k_all = jnp.concatenate([ck.astype(k.dtype), k], axis=1)
                v_all = jnp.concatenate([cv.astype(v.dtype), v], axis=1)
                return mix_by_denominator([dilated_attn_sample(q, k_all, v_all, rel_bias, w, d) for w, d in DILATION_PATTERNS])

            ms, k_s, v_s, vn_s = ab_layer(xs, attend_sample, *ab_w)
            a_k_p.append(k_p[:, -win_p:])
            a_v_p.append(v_p[:, -win_p:])
            a_k_s.append(k_s)
            a_v_s.append(v_s)
            b_v_s.append(vn_s)
        else:
            c_w = (c_mu[j], c_w_r[j], c_w_k[j], c_w_v[j], c_w_g[j], c_w_o[j], c_w0[j], c_w1[j], c_w2[j],
                   c_a0[j], c_a1[j], c_a2[j], c_k_k[j], c_k_a[j], c_r_k[j], c_gn_g[j], c_gn_b[j])
            mp, S_p, sh_p = rwkv7_time_mix(xp, jnp.zeros((Bp, D_MODEL), xp.dtype),
                                           jnp.zeros((Bp, C_HEADS, C_HEAD_DIM, C_HEAD_DIM), jnp.float32), *c_w)
            ms, S_s, sh_s = rwkv7_time_mix(xs, state_c_shift[j], state_c_wkv[j], *c_w)
            c_S_p.append(S_p)
            c_x_p.append(sh_p)
            c_S_s.append(S_s)
            c_x_s.append(sh_s)
        hp = hp + mp
        hs = hs + ms
        hp = hp + per_layer_input(hp, p_prompt[i], ple_w_proj[i], ple_w_gate[i])
        hs = hs + per_layer_input(hs, p_sample[i], ple_w_proj[i], ple_w_gate[i])

    y_prompt = rms_norm(hp, final_norm_g)
    y_sample = rms_norm(hs, final_norm_g)
    return (y_prompt, y_sample, jnp.stack(a_k_p), jnp.stack(a_v_p), jnp.stack(a_k_s), jnp.stack(a_v_s),
            jnp.stack(b_v_s), jnp.stack(c_S_p), jnp.stack(c_x_p), jnp.stack(c_S_s), jnp.stack(c_x_s))
```

```python
import functools
import math

import jax
import jax.numpy as jnp
from jax import lax
from jax.experimental import pallas as pl
from jax.experimental.pallas import tpu as pltpu

F32 = jnp.float32
BF16 = jnp.bfloat16

LANES = 128
SUBLANES = 8
VMEM_BUDGET_BYTES = 56 * 1024 * 1024

A_HEAD_DIM = 128
DILATION_PATTERNS = ((128, 1), (512, 4), (2048, 16))
QBLK = 128
REL_BUCKETS = 32
REL_MAX_DIST = 2048
CHUNK = 128
C_HEAD_DIM = 64
RWKV_CHUNK = 64
RMS_EPS = 1e-6
LN_EPS = 1e-5
GN_EPS = 64e-5
NEG_INF = -1e30


def _params(semantics, vmem_bytes):
    return pltpu.CompilerParams(dimension_semantics=semantics, vmem_limit_bytes=int(vmem_bytes))


def _resident(shape, index_map):
    return pl.BlockSpec(shape, index_map, pipeline_mode=pl.Buffered(1))


def _bf16_terms(x, n):
    if x.dtype == BF16 or n == 1:
        return [x.astype(BF16)]
    terms, rest = [], x
    for _ in range(n):
        terms.append(rest.astype(BF16))
        rest = rest - terms[-1].astype(F32)
    return terms


def _dot_dims(a, b, dims, terms):
    a_terms = _bf16_terms(a, terms[0])
    b_terms = _bf16_terms(b, terms[1])
    out = None
    for i, at in enumerate(a_terms):
        for j, bt in enumerate(b_terms):
            if i + j < max(len(a_terms), len(b_terms)):
                part = lax.dot_general(at, bt, (dims, ((), ())), preferred_element_type=F32)
                out = part if out is None else out + part
    return out


def _dot(a, b, terms=(1, 1)):
    return _dot_dims(a, b, ((1,), (0,)), terms)


def _dot_nt(a, b, terms=(1, 1)):
    return _dot_dims(a, b, ((1,), (1,)), terms)


def _dot_tn(a, b, terms=(1, 1)):
    return _dot_dims(a, b, ((0,), (0,)), terms)


def _rms_norm(x, g):
    return x * lax.rsqrt(jnp.mean(x * x, axis=-1, keepdims=True) + RMS_EPS) * g


def _layer_norm(x, g, b):
    mu = jnp.mean(x, axis=-1, keepdims=True)
    var = jnp.mean(jnp.square(x - mu), axis=-1, keepdims=True)
    return (x - mu) * lax.rsqrt(var + LN_EPS) * g + b


def _silu(x):
    return x * jax.nn.sigmoid(x)


def _ab_in_body(x_ref, g_ref, w_ref, lng_ref, lnb_ref,
                q_ref, k_ref, v_ref, ga_ref, ub_ref, vn_ref, gb_ref, xn_scr):
    j = pl.program_id(1)

    @pl.when(j == 0)
    def _():
        xn_scr[...] = _rms_norm(x_ref[...], g_ref[...]).astype(BF16)

    z = _dot(xn_scr[...], w_ref[...])
    epilogues = (
        (q_ref, lambda t: t),
        (k_ref, lambda t: t),
        (v_ref, lambda t: t),
        (ga_ref, _silu),
        (ub_ref, jax.nn.gelu),
        (vn_ref, lambda t: _layer_norm(jax.nn.gelu(t), lng_ref[...], lnb_ref[...])),
        (gb_ref, _silu),
    )
    for idx, (ref, fn) in enumerate(epilogues):
        @pl.when(j == idx)
        def _(ref=ref, fn=fn):
            ref[...] = fn(z).astype(ref.dtype)


def _ab_in_proj(x, norm_g, w_in_bf16, ln_g, ln_b, tm):
    m, d = x.shape
    width = w_in_bf16.shape[1] // 7
    out = jax.ShapeDtypeStruct((m, width), F32)
    row = lambda i, j: (i, 0)
    vmem = 2 * tm * d * 4 + tm * d * 2 + 2 * d * width * 2 + 7 * 2 * tm * width * 4 + 4 * tm * width * 4
    return pl.pallas_call(
        _ab_in_body,
        grid=(m // tm, 7),
        in_specs=[
            pl.BlockSpec((tm, d), row),
            pl.BlockSpec((1, d), lambda i, j: (0, 0)),
            pl.BlockSpec((d, width), lambda i, j: (0, j)),
            pl.BlockSpec((1, width), lambda i, j: (0, 0)),
            pl.BlockSpec((1, width), lambda i, j: (0, 0)),
        ],
        out_specs=[pl.BlockSpec((tm, width), row)] * 7,
        out_shape=[out] * 7,
        scratch_shapes=[pltpu.VMEM((tm, d), BF16)],
        compiler_params=_params(("parallel", "arbitrary"), vmem),
        name="ab_in_proj",
    )(x, norm_g.reshape(1, d), w_in_bf16, ln_g.reshape(1, width), ln_b.reshape(1, width))


def _t5_bucket(dist):
    n_exact = REL_BUCKETS // 2
    d = jnp.maximum(dist, 1).astype(F32)
    log_b = n_exact + (jnp.log(d / n_exact) / math.log(REL_MAX_DIST / n_exact) * (REL_BUCKETS - n_exact)).astype(jnp.int32)
    return jnp.where(dist < n_exact, dist, jnp.minimum(log_b, REL_BUCKETS - 1))


def _prompt_bias(rel_bias):
    i = jnp.arange(QBLK)[:, None]
    j = jnp.arange(2 * QBLK)[None, :]
    steps = QBLK + i - j
    tables = []
    for window, dil in DILATION_PATTERNS:
        band = (steps >= 0) & (steps <= window // dil)
        bias = jnp.moveaxis(rel_bias[_t5_bucket(jnp.clip(steps, 0) * dil)], -1, 0).astype(F32)
        tables.append(jnp.where(band[None], bias, NEG_INF))
    return jnp.stack(tables)


def _sample_bias(rel_bias):
    back = QBLK - jnp.arange(QBLK)
    cache = jnp.stack([rel_bias[_t5_bucket(back * dil)].T for _, dil in DILATION_PATTERNS]).astype(F32)
    own = rel_bias[_t5_bucket(jnp.zeros((), jnp.int32))].astype(F32)
    return cache, own


def _attn_prompt_body(q_ref, k_ref, v_ref, gate_ref, bias_ref, o_ref, o_scr, lse_scr, *, seq):
    scale = A_HEAD_DIM ** -0.5

    def block(p, dil, start, with_prev):
        rows = pl.ds(start, QBLK, stride=dil) if dil > 1 else pl.ds(start, QBLK)
        q = q_ref[rows, :].astype(BF16)
        k = k_ref[rows, :]
        v = v_ref[rows, :]
        if with_prev:
            prev_start = start - QBLK * dil
            prev = pl.ds(prev_start, QBLK, stride=dil) if dil > 1 else pl.ds(prev_start, QBLK)
            k = jnp.concatenate([k_ref[prev, :], k], axis=0)
            v = jnp.concatenate([v_ref[prev, :], v], axis=0)
            bias = bias_ref[p]
        else:
            bias = bias_ref[p, :, QBLK:]
        s = _dot_nt(q, k.astype(BF16)) * scale + bias
        m = jnp.max(s, axis=-1, keepdims=True)
        e = jnp.exp(s - m)
        l = jnp.sum(e, axis=-1, keepdims=True)
        o = _dot(e.astype(BF16), v.astype(BF16)) / l
        o_scr[p, rows, :] = o
        lse_scr[p, rows, :] = jnp.broadcast_to(m + jnp.log(l), (QBLK, A_HEAD_DIM))

    for p, (_, dil) in enumerate(DILATION_PATTERNS):
        nb = seq // (QBLK * dil)
        shift = dil.bit_length() - 1

        def first(r, carry, p=p, dil=dil):
            block(p, dil, r, False)
            return carry

        lax.fori_loop(0, dil, first, 0)
        if nb > 1:
            def rest(idx, carry, p=p, dil=dil, shift=shift):
                r = idx & (dil - 1)
                n = (idx >> shift) + 1
                block(p, dil, n * (QBLK * dil) + r, True)
                return carry

            lax.fori_loop(0, dil * (nb - 1), rest, 0)

    lse = [lse_scr[p] for p in range(len(DILATION_PATTERNS))]
    m = functools.reduce(jnp.maximum, lse)
    e = [jnp.exp(t - m) for t in lse]
    den = functools.reduce(jnp.add, e)
    mix = functools.reduce(jnp.add, [(e[p] / den) * o_scr[p] for p in range(len(e))])
    o_ref[...] = (mix * gate_ref[...]).astype(o_ref.dtype)


def _attn_prompt(q, k, v, gate, bias, batch, seq):
    width = q.shape[1]
    heads = width // A_HEAD_DIM
    n_pat = len(DILATION_PATTERNS)
    blk = pl.BlockSpec((None, seq, A_HEAD_DIM), lambda b, h: (b, 0, h))
    r3 = lambda t: t.reshape(batch, seq, width)
    vmem = 4 * 2 * seq * A_HEAD_DIM * 4 + 2 * seq * A_HEAD_DIM * 2 + 2 * n_pat * seq * A_HEAD_DIM * 4 \
        + 2 * n_pat * QBLK * 2 * QBLK * 4 + (8 << 20)
    out = pl.pallas_call(
        functools.partial(_attn_prompt_body, seq=seq),
        grid=(batch, heads),
        in_specs=[blk, blk, blk, blk,
                  pl.BlockSpec((n_pat, None, QBLK, 2 * QBLK), lambda b, h: (0, h, 0, 0))],
        out_specs=blk,
        out_shape=jax.ShapeDtypeStruct((batch, seq, width), BF16),
        scratch_shapes=[pltpu.VMEM((n_pat, seq, A_HEAD_DIM), F32), pltpu.VMEM((n_pat, seq, A_HEAD_DIM), F32)],
        compiler_params=_params(("parallel", "parallel"), vmem),
        name="attn_prompt",
    )(r3(q), r3(k), r3(v), r3(gate), bias)
    return out.reshape(batch * seq, width)


def _attn_sample_body(q_ref, kn_ref, vn_ref, gate_ref, bc_ref, bo_ref, *rest):
    n_pat = len(DILATION_PATTERNS)
    kc_refs, vc_refs, o_ref = rest[:n_pat], rest[n_pat:2 * n_pat], rest[2 * n_pat]
    scale = A_HEAD_DIM ** -0.5
    width = q_ref.shape[-1]
    heads = width // A_HEAD_DIM
    head_of_lane = lax.broadcasted_iota(jnp.int32, (heads, width), 1) // A_HEAD_DIM
    own = head_of_lane == lax.broadcasted_iota(jnp.int32, (heads, width), 0)
    rnd = lambda t: t.astype(BF16).astype(F32)
    qm = jnp.where(own, jnp.broadcast_to(rnd(q_ref[...]), (heads, width)), 0.0)
    kn = rnd(kn_ref[...])
    vn = rnd(vn_ref[...])
    s_new = jnp.sum(qm * kn, axis=-1, keepdims=True) * scale + bo_ref[...]
    outs, lses = [], []
    for p in range(n_pat):
        kc = kc_refs[p][...].astype(BF16)
        vc = vc_refs[p][...].astype(BF16)
        s = _dot_nt(qm.astype(BF16), kc) * scale + bc_ref[p]
        m = jnp.maximum(jnp.max(s, axis=-1, keepdims=True), s_new)
        e = jnp.exp(s - m)
        e_new = jnp.exp(s_new - m)
        l = jnp.sum(e, axis=-1, keepdims=True) + e_new
        o = (_dot(e.astype(BF16), vc) + rnd(e_new) * vn) / l
        outs.append(jnp.sum(jnp.where(own, o, 0.0), axis=0, keepdims=True))
        lses.append(jnp.sum(jnp.where(own, m + jnp.log(l), 0.0), axis=0, keepdims=True))
    m = functools.reduce(jnp.maximum, lses)
    e = [jnp.exp(t - m) for t in lses]
    den = functools.reduce(jnp.add, e)
    mix = functools.reduce(jnp.add, [(e[p] / den) * outs[p] for p in range(n_pat)])
    o_ref[...] = (mix * gate_ref[...]).astype(o_ref.dtype)


def _attn_sample(q, k_new, v_new, gate, cache_k, cache_v, bias_cache, bias_own):
    nb, width = q.shape
    heads = width // A_HEAD_DIM
    past = cache_k.shape[1]
    n_pat = len(DILATION_PATTERNS)
    row = pl.BlockSpec((None, 1, width), lambda b: (b, 0, 0))
    r3 = lambda t: t.reshape(nb, 1, width)
    cache_specs, cache_args = [], []
    for cache in (cache_k, cache_v):
        flat = cache.reshape(nb, past, width)
        for window, dil in DILATION_PATTERNS:
            last = past // (QBLK * dil) - 1
            cache_specs.append(pl.BlockSpec((None, QBLK, width), lambda b, last=last: (b, last, 0)))
            cache_args.append(flat.reshape(nb, past // dil, dil * width))
    vmem = 2 * 2 * n_pat * QBLK * width * 4 + (8 << 20)
    out = pl.pallas_call(
        _attn_sample_body,
        grid=(nb,),
        in_specs=[row, row, row, row,
                  pl.BlockSpec((n_pat, heads, QBLK), lambda b: (0, 0, 0)),
                  pl.BlockSpec((heads, 1), lambda b: (0, 0))] + cache_specs,
        out_specs=row,
        out_shape=jax.ShapeDtypeStruct((nb, 1, width), BF16),
        compiler_params=_params(("parallel",), vmem),
        name="attn_sample",
    )(r3(q), r3(k_new), r3(v_new), r3(gate), bias_cache, bias_own.reshape(heads, 1), *cache_args)
    return out.reshape(nb, width)


def _gmlp_prompt_body(vn_ref, ub_ref, gb_ref, ws_ref, bs_ref, o_ref):
    tm = vn_ref.shape[0]
    groups = ws_ref.shape[0]
    ii = lax.broadcasted_iota(jnp.int32, (CHUNK, CHUNK), 0)
    jj = lax.broadcasted_iota(jnp.int32, (CHUNK, CHUNK), 1)
    for g in range(groups):
        w = jnp.where(jj <= ii, ws_ref[g], 0.0).astype(BF16)
        cols = slice(g * LANES, (g + 1) * LANES)
        for c in range(tm // CHUNK):
            rows = slice(c * CHUNK, (c + 1) * CHUNK)
            s = _dot(w, vn_ref[rows, cols].astype(BF16)) + bs_ref[:, cols]
            o_ref[rows, cols] = (ub_ref[rows, cols] * s * gb_ref[rows, cols]).astype(o_ref.dtype)


def _gmlp_prompt(vn, ub, gb, w_s, b_s, tm):
    m, width = vn.shape
    groups = w_s.shape[0]
    bs_rows = jnp.repeat(b_s.T, width // groups, axis=1)
    row = pl.BlockSpec((tm, width), lambda i: (i, 0))
    vmem = 3 * 2 * tm * width * 4 + 2 * tm * width * 2 + (8 << 20)
    return pl.pallas_call(
        _gmlp_prompt_body,
        grid=(m // tm,),
        in_specs=[row, row, row,
                  pl.BlockSpec((groups, CHUNK, CHUNK), lambda i: (0, 0, 0)),
                  pl.BlockSpec((CHUNK, width), lambda i: (0, 0))],
        out_specs=row,
        out_shape=jax.ShapeDtypeStruct((m, width), BF16),
        compiler_params=_params(("parallel",), vmem),
        name="gmlp_prompt",
    )(vn, ub, gb, w_s, bs_rows)


def _gmlp_sample_body(vn_ref, ub_ref, gb_ref, w0_ref, b0_ref, o_ref):
    s = w0_ref[...].astype(BF16).astype(F32) * vn_ref[...].astype(BF16).astype(F32) + b0_ref[...]
    o_ref[...] = (ub_ref[...] * s * gb_ref[...]).astype(o_ref.dtype)


def _gmlp_sample(vn, ub, gb, w_s, b_s):
    m, width = vn.shape
    groups = w_s.shape[0]
    w0 = jnp.repeat(w_s[:, 0, 0], width // groups).reshape(1, width)
    b0 = jnp.repeat(b_s[:, 0], width // groups).reshape(1, width)
    return pl.pallas_call(
        _gmlp_sample_body,
        out_shape=jax.ShapeDtypeStruct((m, width), BF16),
        name="gmlp_sample",
    )(vn, ub, gb, w0, b0)


def _out_ple_body(*refs, n_a, final_norm):
    a_refs = refs[:n_a]
    x_ref, p_ref, wo_ref, wg_ref, wp_ref = refs[n_a:n_a + 5]
    rest = refs[n_a + 5:]
    h = x_ref[...]
    off = 0
    for a_ref in a_refs:
        ka = a_ref.shape[1]
        h = h + _dot(a_ref[...], wo_ref[off:off + ka, :])
        off += ka
    gate = jax.nn.sigmoid(_dot(h.astype(BF16), wg_ref[...]))
    h = h + gate * _dot(p_ref[...].astype(BF16), wp_ref[...])
    if final_norm:
        fg_ref, o_ref = rest
        o_ref[...] = _rms_norm(h, fg_ref[...])
    else:
        (o_ref,) = rest
        o_ref[...] = h


def _out_ple(a_list, x, p, w_out_bf16, w_gate_bf16, w_proj_bf16, final_g, tm):
    m, d = x.shape
    pd = p.shape[1]
    row = lambda i: (i, 0)
    fixed = lambda i: (0, 0)
    in_specs = [pl.BlockSpec((tm, a.shape[1]), row) for a in a_list]
    in_specs += [pl.BlockSpec((tm, d), row), pl.BlockSpec((tm, pd), row),
                 _resident(w_out_bf16.shape, fixed), _resident((d, d), fixed), _resident((pd, d), fixed)]
    args = list(a_list) + [x, p, w_out_bf16, w_gate_bf16, w_proj_bf16]
    if final_g is not None:
        in_specs.append(pl.BlockSpec((1, d), fixed))
        args.append(final_g.reshape(1, d))
    vmem = (w_out_bf16.size + d * d + pd * d) * 2 + 2 * tm * (2 * d * 4 + pd * 4 + w_out_bf16.shape[0] * 2) \
        + 4 * tm * d * 4 + (6 << 20)
    return pl.pallas_call(
        functools.partial(_out_ple_body, n_a=len(a_list), final_norm=final_g is not None),
        grid=(m // tm,),
        in_specs=in_specs,
        out_specs=pl.BlockSpec((tm, d), row),
        out_shape=jax.ShapeDtypeStruct((m, d), F32),
        compiler_params=_params(("parallel",), vmem),
        name="out_ple",
    )(*args)


def _shifted_inputs(h_ref, prev_ref, g_ref, *, seq_rows, prev_given):
    xn = _rms_norm(h_ref[...], g_ref[...])
    if prev_given:
        return xn, prev_ref[...]
    tm = xn.shape[0]
    before = _rms_norm(prev_ref[...], g_ref[...])[SUBLANES - 1:SUBLANES, :]
    starts_sequence = (pl.program_id(0) * tm) % seq_rows == 0
    before = jnp.where(starts_sequence, 0.0, before)
    first = lax.broadcasted_iota(jnp.int32, xn.shape, 0) == 0
    return xn, jnp.where(first, before, pltpu.roll(xn, 1, 0))


def _mix_mm_body(h_ref, prev_ref, g_ref, mu_ref, w_ref, o_ref, *, seq_rows, prev_given, act):
    xn, x_prev = _shifted_inputs(h_ref, prev_ref, g_ref, seq_rows=seq_rows, prev_given=prev_given)
    xm = xn + (x_prev - xn) * mu_ref[...]
    z = _dot(xm.astype(BF16), w_ref[...])
    o_ref[...] = act(z) if act is not None else z


def _lora_body(h_ref, prev_ref, g_ref, muw_ref, mua_ref, w1_ref, w2_ref, w0_ref, a1_ref, a2_ref, a0_ref,
               lw_ref, a_ref, *, seq_rows, prev_given):
    xn, x_prev = _shifted_inputs(h_ref, prev_ref, g_ref, seq_rows=seq_rows, prev_given=prev_given)
    xx = x_prev - xn
    xw = (xn + xx * muw_ref[...]).astype(BF16)
    xa = (xn + xx * mua_ref[...]).astype(BF16)
    zw = w0_ref[...] + _dot(jnp.tanh(_dot(xw, w1_ref[...])).astype(BF16), w2_ref[...])
    w_log = -jax.nn.softplus(-zw) - 0.5
    lw_ref[...] = -jnp.exp(w_log)
    za = a0_ref[...] + _dot(_dot(xa, a1_ref[...]).astype(BF16), a2_ref[...])
    a_ref[...] = jax.nn.sigmoid(za)


def _shift_specs(h, prev, tm, prev_given):
    m, d = h.shape
    row = lambda i: (i, 0)
    if prev_given:
        return [pl.BlockSpec((tm, d), row), pl.BlockSpec((tm, d), row)], [h, prev]
    per = tm // SUBLANES
    return ([pl.BlockSpec((tm, d), row), pl.BlockSpec((SUBLANES, d), lambda i: (jnp.maximum(i * per - 1, 0), 0))],
            [h, h])


def _mix_mm(h, prev, norm_g, mu_row, w_bf16, act, tm, seq_rows):
    m, d = h.shape
    n = w_bf16.shape[1]
    prev_given = prev is not None
    specs, args = _shift_specs(h, prev, tm, prev_given)
    fixed = lambda i: (0, 0)
    vmem = d * n * 2 + 4 * tm * d * 4 + 2 * tm * n * 4 + 4 * tm * d * 4 + (6 << 20)
    return pl.pallas_call(
        functools.partial(_mix_mm_body, seq_rows=seq_rows, prev_given=prev_given, act=act),
        grid=(m // tm,),
        in_specs=specs + [pl.BlockSpec((1, d), fixed), pl.BlockSpec((1, d), fixed), _resident((d, n), fixed)],
        out_specs=pl.BlockSpec((tm, n), lambda i: (i, 0)),
        out_shape=jax.ShapeDtypeStruct((m, n), F32),
        compiler_params=_params(("parallel",), vmem),
        name="rwkv_mix_mm",
    )(*args, norm_g.reshape(1, d), mu_row.reshape(1, d), w_bf16)


def _pad_lora(w_down, w_up):
    rank = w_down.shape[1]
    pad = (-rank) % LANES
    return (jnp.pad(w_down, ((0, 0), (0, pad))).astype(BF16), jnp.pad(w_up, ((0, pad), (0, 0))).astype(BF16))


def _lora(h, prev, norm_g, mu_w, mu_a, w1, w2, w0, a1, a2, a0, tm, seq_rows):
    m, d = h.shape
    prev_given = prev is not None
    specs, args = _shift_specs(h, prev, tm, prev_given)
    w1p, w2p = _pad_lora(w1, w2)
    a1p, a2p = _pad_lora(a1, a2)
    rank = w1p.shape[1]
    fixed = lambda i: (0, 0)
    vec = pl.BlockSpec((1, d), fixed)
    down = pl.BlockSpec((d, rank), fixed)
    up = pl.BlockSpec((rank, d), fixed)
    out = jax.ShapeDtypeStruct((m, d), F32)
    vmem = (2 + 4 + 10) * tm * d * 4 + 8 * d * rank * 2 + (4 << 20)
    return pl.pallas_call(
        functools.partial(_lora_body, seq_rows=seq_rows, prev_given=prev_given),
        grid=(m // tm,),
        in_specs=specs + [vec, vec, vec, down, up, vec, down, up, vec],
        out_specs=[pl.BlockSpec((tm, d), lambda i: (i, 0))] * 2,
        out_shape=[out, out],
        compiler_params=_params(("parallel",), vmem),
        name="rwkv_lora",
    )(*args, norm_g.reshape(1, d), mu_w.reshape(1, d), mu_a.reshape(1, d),
      w1p, w2p, w0.reshape(1, d), a1p, a2p, a0.reshape(1, d))


def _rwkv_prompt_body(r_ref, k_ref, v_ref, lw_ref, a_ref, g_ref, kk_ref, ka_ref, rk_ref, gng_ref, gnb_ref,
                      o_ref, st_ref, q2_scr, y0_scr, m_scr, h0_scr, *, seq, group):
    c_len = RWKV_CHUNK
    n_chunks = seq // c_len
    hd = C_HEAD_DIM
    lane = lax.broadcasted_iota(jnp.int32, (1, LANES), 1)
    head_masks = (lane < hd, lane >= hd)
    ti = lax.broadcasted_iota(jnp.int32, (c_len, c_len), 0)
    si = lax.broadcasted_iota(jnp.int32, (c_len, c_len), 1)
    incl = ti >= si
    strict = ti > si
    eye_c = (ti == si).astype(F32)
    tri = incl.astype(F32)
    bi = lax.broadcasted_iota(jnp.int32, (LANES, LANES), 0)
    bj = lax.broadcasted_iota(jnp.int32, (LANES, LANES), 1)
    same_head = (bi // hd) == (bj // hd)
    head_ones = same_head.astype(BF16)
    eye_l = (bi == bj).astype(F32)

    def head_sum(x):
        return _dot(x, head_ones, (2, 1))

    def load(c):
        rows = pl.ds(pl.multiple_of(c * c_len, c_len), c_len)
        r, k, v, a = r_ref[rows, :], k_ref[rows, :], v_ref[rows, :], a_ref[rows, :]
        k2 = k * (1.0 + (a - 1.0) * ka_ref[...])
        return rows, r, k, k2, v, a

    def prepare(c):
        rows, r, k, k2, v, a = load(c)
        lw = lw_ref[rows, :]
        kk = k * kk_ref[...]
        kk = kk / jnp.maximum(jnp.sqrt(head_sum(kk * kk)), 1e-12)
        cum = _dot(tri, lw, (1, 3))
        last = cum[c_len - 1:c_len, :]
        b = kk * a
        ekk = kk * jnp.exp(cum - lw)
        er = r * jnp.exp(cum)
        inv = jnp.exp(-cum)
        eb = b * inv
        ek = k2 * inv
        to_end = jnp.exp(last - cum)
        eb_end = b * to_end
        ek_end = k2 * to_end
        q1, u0, q2, y0 = [], [], [], []
        for mask in head_masks:
            lhs = jnp.concatenate([jnp.where(mask, ekk, 0.0), jnp.where(mask, er, 0.0)], axis=0)
            gb = _dot_nt(lhs, eb)
            gk = _dot_nt(lhs, ek)
            l_b = jnp.where(strict, gb[:c_len], 0.0)
            a_b = jnp.where(incl, gb[c_len:], 0.0)
            l_k = jnp.where(strict, gk[:c_len], 0.0)
            a_k = jnp.where(incl, gk[c_len:], 0.0)
            inv_t = eye_c - l_b
            power = l_b
            for _ in range(int(math.log2(c_len)) - 1):
                power = _dot(power, power)
                inv_t = inv_t + _dot(inv_t, power)
            tz = _dot(inv_t, jnp.concatenate([ekk, _dot(l_k, v)], axis=1))
            q1_h, u0_h = tz[:, :LANES], -tz[:, LANES:]
            ab = _dot(a_b, jnp.concatenate([q1_h, u0_h], axis=1))
            q1.append(q1_h)
            u0.append(u0_h)
            q2.append(er - ab[:, :LANES])
            y0.append(ab[:, LANES:] + _dot(a_k, v))
        pick = lambda pair: jnp.where(head_masks[0], pair[0], pair[1])
        q1, u0, q2, y0 = pick(q1), pick(u0), pick(q2), pick(y0)
        bq = _dot_tn(eb_end, jnp.concatenate([q1, u0], axis=1))
        kv = _dot_tn(ek_end, v)
        q2_scr[c] = q2
        y0_scr[c] = y0
        m_scr[c] = jnp.where(same_head, eye_l * jnp.exp(last) - bq[:, :LANES], 0.0)
        h0_scr[c] = jnp.where(same_head, bq[:, LANES:] + kv, 0.0)

    def prepare_group(gi, carry):
        for u in range(group):
            prepare(gi * group + u)
        return carry

    lax.fori_loop(0, n_chunks // group, prepare_group, 0)

    def advance(c, state):
        rows, r, _, k2, v, _ = load(c)
        y = _dot(q2_scr[c], state, (2, 2)) + y0_scr[c]
        state = _dot(m_scr[c], state, (2, 2)) + h0_scr[c]
        mean = head_sum(y) * (1.0 / hd)
        dev = y - mean
        var = head_sum(dev * dev) * (1.0 / hd)
        yn = dev * lax.rsqrt(var + GN_EPS) * gng_ref[...] + gnb_ref[...]
        bonus = head_sum(r * k2 * rk_ref[...]) * v
        o_ref[rows, :] = ((yn + bonus) * g_ref[rows, :]).astype(o_ref.dtype)
        return state

    state = lax.fori_loop(0, n_chunks, advance, jnp.zeros((LANES, LANES), F32))
    st_ref[...] = _dot_tn(state, eye_l, (3, 1))


def _rwkv_prompt(r, k, v, lw, a, g, k_k, k_a, r_k, gn_g, gn_b, batch, seq, group=2):
    d = r.shape[1]
    pairs = d // LANES
    n_chunks = seq // RWKV_CHUNK
    blk = pl.BlockSpec((None, seq, LANES), lambda b, p: (b, 0, p))
    vec = pl.BlockSpec((1, LANES), lambda b, p: (0, p))
    r3 = lambda t: t.reshape(batch, seq, d)
    v2 = lambda t: t.reshape(1, d)
    vmem = 2 * 6 * seq * LANES * 4 + 2 * seq * LANES * 2 + n_chunks * (2 * RWKV_CHUNK + 2 * LANES) * LANES * 4 + (12 << 20)
    out, state = pl.pallas_call(
        functools.partial(_rwkv_prompt_body, seq=seq, group=group),
        grid=(batch, pairs),
        in_specs=[blk] * 6 + [vec] * 5,
        out_specs=[blk, pl.BlockSpec((None, None, LANES, LANES), lambda b, p: (b, p, 0, 0))],
        out_shape=[jax.ShapeDtypeStruct((batch, seq, d), BF16),
                   jax.ShapeDtypeStruct((batch, pairs, LANES, LANES), F32)],
        scratch_shapes=[pltpu.VMEM((n_chunks, RWKV_CHUNK, LANES), F32), pltpu.VMEM((n_chunks, RWKV_CHUNK, LANES), F32),
                        pltpu.VMEM((n_chunks, LANES, LANES), F32), pltpu.VMEM((n_chunks, LANES, LANES), F32)],
        compiler_params=_params(("parallel", "parallel"), vmem),
        name="rwkv_prompt",
    )(r3(r), r3(k), r3(v), r3(lw), r3(a), r3(g), v2(k_k), v2(k_a), v2(r_k), v2(gn_g), v2(gn_b))
    hd = C_HEAD_DIM
    per_pair = LANES // hd
    heads = [state[:, :, i * hd:(i + 1) * hd, i * hd:(i + 1) * hd] for i in range(per_pair)]
    return out.reshape(batch * seq, d), jnp.stack(heads, axis=2).reshape(batch, pairs * per_pair, hd, hd)


def _rwkv_sample_body(s_ref, r_ref, k_ref, lw_ref, a_ref, v_ref, kk_ref, ka_ref, rk_ref, gng_ref, gnb_ref,
                      y_ref, so_ref):
    r, k, a = r_ref[...], k_ref[...], a_ref[...]
    v = v_ref[...]
    kk = k * kk_ref[...]
    kk = kk / jnp.maximum(jnp.sqrt(jnp.sum(kk * kk, axis=-1, keepdims=True)), 1e-12)
    k2 = k * (1.0 + (a - 1.0) * ka_ref[...])
    s = s_ref[...]
    sa = -jnp.sum(s * kk, axis=-1, keepdims=True)
    s = s * jnp.exp(lw_ref[...]) + sa * (kk * a) + v * k2
    so_ref[...] = s
    y = jnp.sum(s * r, axis=-1, keepdims=True)
    mean = jnp.mean(y, axis=1, keepdims=True)
    var = jnp.mean(jnp.square(y - mean), axis=1, keepdims=True)
    y = (y - mean) * lax.rsqrt(var + GN_EPS) * gng_ref[...] + gnb_ref[...]
    bonus = jnp.sum(r * k2 * rk_ref[...], axis=-1, keepdims=True)
    y_ref[...] = y + bonus * v


def _rwkv_sample(state, r, k, v, lw, a, k_k, k_a, r_k, gn_g, gn_b):
    nb, heads, hd, _ = state.shape
    rowv = lambda t: t.reshape(nb, heads, 1, hd)
    rowp = lambda t: t.reshape(heads, 1, hd)
    colp = lambda t: t.reshape(heads, hd, 1)
    st = pl.BlockSpec((None, heads, hd, hd), lambda b: (b, 0, 0, 0))
    rv = pl.BlockSpec((None, heads, 1, hd), lambda b: (b, 0, 0, 0))
    cv = pl.BlockSpec((None, heads, hd, 1), lambda b: (b, 0, 0, 0))
    rp = pl.BlockSpec((heads, 1, hd), lambda b: (0, 0, 0))
    cp = pl.BlockSpec((heads, hd, 1), lambda b: (0, 0, 0))
    y, new_state = pl.pallas_call(
        _rwkv_sample_body,
        grid=(nb,),
        in_specs=[st, rv, rv, rv, rv, cv, rp, rp, rp, cp, cp],
        out_specs=[cv, st],
        out_shape=[jax.ShapeDtypeStruct((nb, heads, hd, 1), F32), jax.ShapeDtypeStruct(state.shape, F32)],
        compiler_params=_params(("parallel",), 32 << 20),
        name="rwkv_sample",
    )(state, rowv(r), rowv(k), rowv(lw), rowv(a), v.reshape(nb, heads, hd, 1),
      rowp(k_k), rowp(k_a), rowp(r_k), colp(gn_g), colp(gn_b))
    return y.reshape(nb, heads * hd), new_state


def _gate_cast_body(y_ref, g_ref, o_ref):
    o_ref[...] = (y_ref[...] * g_ref[...]).astype(o_ref.dtype)


def _gate_cast(y, g):
    return pl.pallas_call(_gate_cast_body, out_shape=jax.ShapeDtypeStruct(y.shape, BF16), name="gate_cast")(y, g)


def kernel(x_prompt, x_sample, cache_a_k, cache_a_v, state_c_wkv, state_c_shift, p_prompt, p_sample, norm_g, final_norm_g, rel_bias, ab_w_in, ab_w_out, b_w_s, b_b_s, b_ln_g, b_ln_b, c_mu, c_w_r, c_w_k, c_w_v, c_w_g, c_w_o, c_w0, c_w1, c_w2, c_a0, c_a1, c_a2, c_k_k, c_k_a, c_r_k, c_gn_g, c_gn_b, ple_w_proj, ple_w_gate):
    batch, seq, d = x_prompt.shape
    nb = x_sample.shape[0]
    assert x_sample.shape[1] == 1 and seq % (QBLK * max(dil for _, dil in DILATION_PATTERNS)) == 0
    assert norm_g.shape[0] == 2, "layer pattern implemented for depth 2: one attention+gMLP layer, one RWKV-7 layer"
    a_heads = cache_a_k.shape[3]
    c_heads = state_c_wkv.shape[2]
    m = batch * seq
    hp = x_prompt.reshape(m, d)
    hs = x_sample.reshape(nb, d)
    pp = p_prompt.reshape(p_prompt.shape[0], m, -1)
    ps = p_sample.reshape(p_sample.shape[0], nb, -1)
    w_gate = ple_w_gate.astype(BF16)
    w_proj = ple_w_proj.astype(BF16)

    w_in = ab_w_in[0].astype(BF16)
    w_out = ab_w_out[0].astype(BF16)
    bias_prompt = _prompt_bias(rel_bias)
    bias_cache, bias_own = _sample_bias(rel_bias)

    q, k, v, ga, ub, vn, gb = _ab_in_proj(hp, norm_g[0], w_in, b_ln_g[0], b_ln_b[0], tm=512)
    oa = _attn_prompt(q, k, v, ga, bias_prompt, batch, seq)
    ob = _gmlp_prompt(vn, ub, gb, b_w_s[0], b_b_s[0], tm=512)
    hp = _out_ple([oa, ob], hp, pp[0], w_out, w_gate[0], w_proj[0], None, tm=256)
    a_k_p = k.reshape(1, batch, seq, a_heads, A_HEAD_DIM)
    a_v_p = v.reshape(1, batch, seq, a_heads, A_HEAD_DIM)

    qs, ks, vs, gas, ubs, vns, gbs = _ab_in_proj(hs, norm_g[0], w_in, b_ln_g[0], b_ln_b[0], tm=nb)
    oas = _attn_sample(qs, ks, vs, gas, cache_a_k[0], cache_a_v[0], bias_cache, bias_own)
    obs = _gmlp_sample(vns, ubs, gbs, b_w_s[0], b_b_s[0])
    hs = _out_ple([oas, obs], hs, ps[0], w_out, w_gate[0], w_proj[0], None, tm=nb)
    a_k_s = ks.reshape(1, nb, 1, a_heads, A_HEAD_DIM)
    a_v_s = vs.reshape(1, nb, 1, a_heads, A_HEAD_DIM)
    b_v_s = vns.reshape(1, nb, 1, -1)

    mu = c_mu[0]
    w_r, w_k, w_v, w_g, w_o = (t[0].astype(BF16) for t in (c_w_r, c_w_k, c_w_v, c_w_g, c_w_o))
    rk_flat = c_r_k[0].reshape(-1)

    def projections(h, prev, tm, seq_rows):
        r = _mix_mm(h, prev, norm_g[1], mu[0], w_r, None, tm, seq_rows)
        kx = _mix_mm(h, prev, norm_g[1], mu[2], w_k, None, tm, seq_rows)
        vx = _mix_mm(h, prev, norm_g[1], mu[3], w_v, None, tm, seq_rows)
        g = _mix_mm(h, prev, norm_g[1], mu[5], w_g, _silu, tm, seq_rows)
        lw, a = _lora(h, prev, norm_g[1], mu[1], mu[4], c_w1[0], c_w2[0], c_w0[0], c_a1[0], c_a2[0], c_a0[0],
                      min(tm, 256), seq_rows)
        return r, kx, vx, g, lw, a

    r, kx, vx, g, lw, a = projections(hp, None, 512, seq)
    yg, s_p = _rwkv_prompt(r, kx, vx, lw, a, g, c_k_k[0], c_k_a[0], rk_flat, c_gn_g[0], c_gn_b[0], batch, seq)
    y_prompt = _out_ple([yg], hp, pp[1], w_o, w_gate[1], w_proj[1], final_norm_g, tm=256)
    last_rows = hp.reshape(batch, seq, d)[:, seq - SUBLANES:, :].reshape(batch * SUBLANES, d)
    sh_p = _norm_rows(last_rows, norm_g[1]).reshape(batch, SUBLANES, d)[:, SUBLANES - 1]

    rs, kxs, vxs, gs, lws, a_s = projections(hs, state_c_shift[0], nb, 1)
    ys, s_s = _rwkv_sample(state_c_wkv[0], rs, kxs, vxs, lws, a_s, c_k_k[0], c_k_a[0], rk_flat, c_gn_g[0], c_gn_b[0])
    y_sample = _out_ple([_gate_cast(ys, gs)], hs, ps[1], w_o, w_gate[1], w_proj[1], final_norm_g, tm=nb)
    sh_s = _norm_rows(hs, norm_g[1])

    return (y_prompt.reshape(batch, seq, d), y_sample.reshape(nb, 1, d), a_k_p, a_v_p, a_k_s, a_v_s, b_v_s,
            s_p[None], sh_p[None], s_s[None], sh_s[None])


def _norm_rows_body(x_ref, g_ref, o_ref):
    o_ref[...] = _rms_norm(x_ref[...], g_ref[...])


def _norm_rows(x, g):
    return pl.pallas_call(_norm_rows_body, out_shape=jax.ShapeDtypeStruct(x.shape, F32), name="norm_rows")(
        x, g.reshape(1, -1))
```

```python
import functools
import math

import jax
import jax.numpy as jnp
from jax import lax
from jax.experimental import pallas as pl
from jax.experimental.pallas import tpu as pltpu

F32 = jnp.float32
BF16 = jnp.bfloat16

LANES = 128
SUBLANES = 8
VMEM_BUDGET_BYTES = 56 * 1024 * 1024

A_HEAD_DIM = 128
DILATION_PATTERNS = ((128, 1), (512, 4), (2048, 16))
QBLK = 128
ATTN_GROUP = 8
REL_BUCKETS = 32
REL_MAX_DIST = 2048
CHUNK = 128
C_HEAD_DIM = 64
RWKV_CHUNK = 64
RMS_EPS = 1e-6
LN_EPS = 1e-5
GN_EPS = 64e-5
NEG_INF = -1e30


def _params(semantics, vmem_bytes):
    return pltpu.CompilerParams(dimension_semantics=semantics, vmem_limit_bytes=int(vmem_bytes))


def _resident(shape, index_map):
    return pl.BlockSpec(shape, index_map, pipeline_mode=pl.Buffered(1))


def _bf16_terms(x, n):
    if x.dtype == BF16 or n == 1:
        return [x.astype(BF16)]
    terms, rest = [], x
    for _ in range(n):
        terms.append(rest.astype(BF16))
        rest = rest - terms[-1].astype(F32)
    return terms


def _dot_dims(a, b, dims, terms):
    a_terms = _bf16_terms(a, terms[0])
    b_terms = _bf16_terms(b, terms[1])
    out = None
    for i, at in enumerate(a_terms):
        for j, bt in enumerate(b_terms):
            if i + j < max(len(a_terms), len(b_terms)):
                part = lax.dot_general(at, bt, (dims, ((), ())), preferred_element_type=F32)
                out = part if out is None else out + part
    return out


def _dot(a, b, terms=(1, 1)):
    return _dot_dims(a, b, ((1,), (0,)), terms)


def _dot_nt(a, b, terms=(1, 1)):
    return _dot_dims(a, b, ((1,), (1,)), terms)


def _dot_tn(a, b, terms=(1, 1)):
    return _dot_dims(a, b, ((0,), (0,)), terms)


def _rms_norm(x, g):
    return x * lax.rsqrt(jnp.mean(x * x, axis=-1, keepdims=True) + RMS_EPS) * g


def _layer_norm(x, g, b):
    mu = jnp.mean(x, axis=-1, keepdims=True)
    var = jnp.mean(jnp.square(x - mu), axis=-1, keepdims=True)
    return (x - mu) * lax.rsqrt(var + LN_EPS) * g + b


def _silu(x):
    return x * jax.nn.sigmoid(x)


def _ab_in_body(x_ref, g_ref, w_ref, lng_ref, lnb_ref,
                q_ref, k_ref, v_ref, ga_ref, ub_ref, vn_ref, gb_ref, xn_scr):
    j = pl.program_id(1)

    @pl.when(j == 0)
    def _():
        xn_scr[...] = _rms_norm(x_ref[...], g_ref[...]).astype(BF16)

    z = _dot(xn_scr[...], w_ref[...])
    epilogues = (
        (q_ref, lambda t: t),
        (k_ref, lambda t: t),
        (v_ref, lambda t: t),
        (ga_ref, _silu),
        (ub_ref, jax.nn.gelu),
        (vn_ref, lambda t: _layer_norm(jax.nn.gelu(t), lng_ref[...], lnb_ref[...])),
        (gb_ref, _silu),
    )
    for idx, (ref, fn) in enumerate(epilogues):
        @pl.when(j == idx)
        def _(ref=ref, fn=fn):
            ref[...] = fn(z).astype(ref.dtype)


def _ab_in_proj(x, norm_g, w_in_bf16, ln_g, ln_b, tm):
    m, d = x.shape
    width = w_in_bf16.shape[1] // 7
    out = jax.ShapeDtypeStruct((m, width), F32)
    row = lambda i, j: (i, 0)
    vmem = 2 * tm * d * 4 + tm * d * 2 + 2 * d * width * 2 + 7 * 2 * tm * width * 4 + 4 * tm * width * 4
    return pl.pallas_call(
        _ab_in_body,
        grid=(m // tm, 7),
        in_specs=[
            pl.BlockSpec((tm, d), row),
            pl.BlockSpec((1, d), lambda i, j: (0, 0)),
            pl.BlockSpec((d, width), lambda i, j: (0, j)),
            pl.BlockSpec((1, width), lambda i, j: (0, 0)),
            pl.BlockSpec((1, width), lambda i, j: (0, 0)),
        ],
        out_specs=[pl.BlockSpec((tm, width), row)] * 7,
        out_shape=[out] * 7,
        scratch_shapes=[pltpu.VMEM((tm, d), BF16)],
        compiler_params=_params(("parallel", "arbitrary"), vmem),
        name="ab_in_proj",
    )(x, norm_g.reshape(1, d), w_in_bf16, ln_g.reshape(1, width), ln_b.reshape(1, width))


def _t5_bucket(dist):
    n_exact = REL_BUCKETS // 2
    d = jnp.maximum(dist, 1).astype(F32)
    log_b = n_exact + (jnp.log(d / n_exact) / math.log(REL_MAX_DIST / n_exact) * (REL_BUCKETS - n_exact)).astype(jnp.int32)
    return jnp.where(dist < n_exact, dist, jnp.minimum(log_b, REL_BUCKETS - 1))


def _bias_at(rel_bias, dist):
    one_hot = jax.nn.one_hot(_t5_bucket(dist), REL_BUCKETS, dtype=F32)
    return jnp.einsum("...k,kh->...h", one_hot, rel_bias.astype(F32), precision=lax.Precision.HIGHEST)


def _prompt_bias(rel_bias):
    i = jnp.arange(QBLK)[:, None]
    j = jnp.arange(2 * QBLK)[None, :]
    steps = QBLK + i - j
    tables = []
    for window, dil in DILATION_PATTERNS:
        band = (steps >= 0) & (steps <= window // dil)
        bias = jnp.moveaxis(_bias_at(rel_bias, jnp.clip(steps, 0) * dil), -1, 0)
        tables.append(jnp.where(band[None], bias, NEG_INF))
    return jnp.stack(tables)


def _sample_bias(rel_bias):
    back = QBLK - jnp.arange(QBLK)
    cache = jnp.stack([_bias_at(rel_bias, back * dil) for _, dil in DILATION_PATTERNS])
    own = _bias_at(rel_bias, jnp.zeros((), jnp.int32))
    return cache, own


def _attn_prompt_body(q_ref, k_ref, v_ref, gate_ref, bias_ref, o_ref, o_scr, lse_scr, *, seq):
    scale = A_HEAD_DIM ** -0.5

    def blocks(p, dil, starts, with_prev):
        ds = lambda st: pl.ds(st, QBLK, stride=dil) if dil > 1 else pl.ds(st, QBLK)
        rows = [ds(st) for st in starts]
        q = [q_ref[rw, :].astype(BF16) for rw in rows]
        k = [k_ref[rw, :] for rw in rows]
        v = [v_ref[rw, :] for rw in rows]
        if with_prev:
            prev = [ds(st - QBLK * dil) for st in starts]
            k = [jnp.concatenate([k_ref[pv, :], x], axis=0) for pv, x in zip(prev, k)]
            v = [jnp.concatenate([v_ref[pv, :], x], axis=0) for pv, x in zip(prev, v)]
            bias = bias_ref[p]
        else:
            bias = bias_ref[p, :, QBLK:]
        s = [_dot_nt(x, y) * scale + bias for x, y in zip(q, k)]
        m = [jnp.max(x, axis=-1, keepdims=True) for x in s]
        e = [jnp.exp(x - y) for x, y in zip(s, m)]
        l = [jnp.sum(x, axis=-1, keepdims=True) for x in e]
        o = [_dot(x, y) / z for x, y, z in zip(e, v, l)]
        for rw, x, y, z in zip(rows, o, m, l):
            o_scr[p, rw, :] = x
            lse_scr[p, rw, :] = jnp.broadcast_to(y + jnp.log(z), (QBLK, A_HEAD_DIM))

    def widest_group(count):
        return max(g for g in range(1, ATTN_GROUP + 1) if count % g == 0)

    for p, (_, dil) in enumerate(DILATION_PATTERNS):
        nb = seq // (QBLK * dil)
        shift = dil.bit_length() - 1
        g_first = widest_group(dil)

        def first(i, carry, p=p, dil=dil, g=g_first):
            blocks(p, dil, [i * g + u for u in range(g)], False)
            return carry

        lax.fori_loop(0, dil // g_first, first, 0)
        if nb > 1:
            g_rest = widest_group(dil * (nb - 1))

            def rest(i, carry, p=p, dil=dil, shift=shift, g=g_rest):
                idx = [i * g + u for u in range(g)]
                blocks(p, dil, [((x >> shift) + 1) * (QBLK * dil) + (x & (dil - 1)) for x in idx], True)
                return carry

            lax.fori_loop(0, dil * (nb - 1) // g_rest, rest, 0)

    lse = [lse_scr[p] for p in range(len(DILATION_PATTERNS))]
    m = functools.reduce(jnp.maximum, lse)
    e = [jnp.exp(t - m) for t in lse]
    den = functools.reduce(jnp.add, e)
    mix = functools.reduce(jnp.add, [(e[p] / den) * o_scr[p] for p in range(len(e))])
    o_ref[...] = (mix * gate_ref[...]).astype(o_ref.dtype)


def _attn_prompt(q, k, v, gate, bias, batch, seq):
    width = q.shape[1]
    heads = width // A_HEAD_DIM
    n_pat = len(DILATION_PATTERNS)
    blk = pl.BlockSpec((None, seq, A_HEAD_DIM), lambda b, h: (b, 0, h))
    r3 = lambda t: t.reshape(batch, seq, width)
    vmem = 4 * 2 * seq * A_HEAD_DIM * 4 + 2 * seq * A_HEAD_DIM * 2 + 2 * n_pat * seq * A_HEAD_DIM * 4 \
        + 2 * n_pat * QBLK * 2 * QBLK * 4 + (8 << 20)
    out = pl.pallas_call(
        functools.partial(_attn_prompt_body, seq=seq),
        grid=(batch, heads),
        in_specs=[blk, blk, blk, blk,
                  pl.BlockSpec((n_pat, None, QBLK, 2 * QBLK), lambda b, h: (0, h, 0, 0))],
        out_specs=blk,
        out_shape=jax.ShapeDtypeStruct((batch, seq, width), BF16),
        scratch_shapes=[pltpu.VMEM((n_pat, seq, A_HEAD_DIM), F32), pltpu.VMEM((n_pat, seq, A_HEAD_DIM), F32)],
        compiler_params=_params(("parallel", "parallel"), vmem),
        name="attn_prompt",
    )(r3(q), r3(k), r3(v), r3(gate), bias)
    return out.reshape(batch * seq, width)


def _attn_sample_body(q_ref, kn_ref, vn_ref, gate_ref, bc_ref, bo_ref, *rest):
    n_pat = len(DILATION_PATTERNS)
    kc_refs, vc_refs, o_ref = rest[:n_pat], rest[n_pat:2 * n_pat], rest[2 * n_pat]
    scale = A_HEAD_DIM ** -0.5
    rnd = lambda t: t.astype(BF16).astype(F32)
    q = rnd(q_ref[...])
    kn = rnd(kn_ref[...])
    vn = rnd(vn_ref[...])
    s_new = jnp.sum(q * kn, axis=-1, keepdims=True) * scale + bo_ref[...]
    outs, lses = [], []
    for p in range(n_pat):
        kc = rnd(kc_refs[p][...])
        vc = rnd(vc_refs[p][...])
        s = jnp.sum(q[None] * kc, axis=-1, keepdims=True) * scale + bc_ref[p]
        m = jnp.maximum(jnp.max(s, axis=0), s_new)
        e = jnp.exp(s - m[None])
        e_new = jnp.exp(s_new - m)
        l = jnp.sum(e, axis=0) + e_new
        outs.append((jnp.sum(rnd(e) * vc, axis=0) + rnd(e_new) * vn) / l)
        lses.append(m + jnp.log(l))
    m = functools.reduce(jnp.maximum, lses)
    e = [jnp.exp(t - m) for t in lses]
    den = functools.reduce(jnp.add, e)
    mix = functools.reduce(jnp.add, [(e[p] / den) * outs[p] for p in range(n_pat)])
    o_ref[...] = (mix * gate_ref[...]).astype(o_ref.dtype)


def _attn_sample(q, k_new, v_new, gate, cache_k, cache_v, bias_cache, bias_own):
    nb, past, heads, hd = cache_k.shape
    n_pat = len(DILATION_PATTERNS)
    row = pl.BlockSpec((None, heads, hd), lambda b: (b, 0, 0))
    r3 = lambda t: t.reshape(nb, heads, hd)
    cache_specs, cache_args = [], []
    for cache in (cache_k, cache_v):
        for window, dil in DILATION_PATTERNS:
            last = past // (QBLK * dil) - 1
            cache_specs.append(pl.BlockSpec((None, QBLK, None, heads, hd), lambda b, last=last: (b, last, 0, 0, 0)))
            cache_args.append(cache.reshape(nb, past // dil, dil, heads, hd))
    vmem = 2 * 2 * n_pat * QBLK * heads * hd * 4 + 8 * QBLK * heads * hd * 4 + (8 << 20)
    out = pl.pallas_call(
        _attn_sample_body,
        grid=(nb,),
        in_specs=[row, row, row, row,
                  pl.BlockSpec((n_pat, QBLK, heads, 1), lambda b: (0, 0, 0, 0)),
                  pl.BlockSpec((heads, 1), lambda b: (0, 0))] + cache_specs,
        out_specs=row,
        out_shape=jax.ShapeDtypeStruct((nb, heads, hd), BF16),
        compiler_params=_params(("parallel",), vmem),
        name="attn_sample",
    )(r3(q), r3(k_new), r3(v_new), r3(gate), bias_cache[..., None], bias_own.reshape(heads, 1), *cache_args)
    return out.reshape(nb, heads * hd)


def _gmlp_prompt_body(vn_ref, ub_ref, gb_ref, ws_ref, bs_ref, o_ref):
    tm = vn_ref.shape[0]
    groups = ws_ref.shape[0]
    ii = lax.broadcasted_iota(jnp.int32, (CHUNK, CHUNK), 0)
    jj = lax.broadcasted_iota(jnp.int32, (CHUNK, CHUNK), 1)
    for g in range(groups):
        w = jnp.where(jj <= ii, ws_ref[g], 0.0).astype(BF16)
        cols = slice(g * LANES, (g + 1) * LANES)
        for c in range(tm // CHUNK):
            rows = slice(c * CHUNK, (c + 1) * CHUNK)
            s = _dot(w, vn_ref[rows, cols].astype(BF16)) + bs_ref[:, cols]
            o_ref[rows, cols] = (ub_ref[rows, cols] * s * gb_ref[rows, cols]).astype(o_ref.dtype)


def _gmlp_prompt(vn, ub, gb, w_s, b_s, tm):
    m, width = vn.shape
    groups = w_s.shape[0]
    bs_rows = jnp.repeat(b_s.T, width // groups, axis=1)
    row = pl.BlockSpec((tm, width), lambda i: (i, 0))
    vmem = 3 * 2 * tm * width * 4 + 2 * tm * width * 2 + (8 << 20)
    return pl.pallas_call(
        _gmlp_prompt_body,
        grid=(m // tm,),
        in_specs=[row, row, row,
                  pl.BlockSpec((groups, CHUNK, CHUNK), lambda i: (0, 0, 0)),
                  pl.BlockSpec((CHUNK, width), lambda i: (0, 0))],
        out_specs=row,
        out_shape=jax.ShapeDtypeStruct((m, width), BF16),
        compiler_params=_params(("parallel",), vmem),
        name="gmlp_prompt",
    )(vn, ub, gb, w_s, bs_rows)


def _gmlp_sample_body(vn_ref, ub_ref, gb_ref, w0_ref, b0_ref, o_ref):
    s = w0_ref[...].astype(BF16).astype(F32) * vn_ref[...].astype(BF16).astype(F32) + b0_ref[...]
    o_ref[...] = (ub_ref[...] * s * gb_ref[...]).astype(o_ref.dtype)


def _gmlp_sample(vn, ub, gb, w_s, b_s):
    m, width = vn.shape
    groups = w_s.shape[0]
    w0 = jnp.repeat(w_s[:, 0, 0], width // groups).reshape(1, width)
    b0 = jnp.repeat(b_s[:, 0], width // groups).reshape(1, width)
    return pl.pallas_call(
        _gmlp_sample_body,
        out_shape=jax.ShapeDtypeStruct((m, width), BF16),
        name="gmlp_sample",
    )(vn, ub, gb, w0, b0)


def _out_ple_body(*refs, n_a, final_norm):
    a_refs = refs[:n_a]
    x_ref, p_ref, wo_ref, wg_ref, wp_ref = refs[n_a:n_a + 5]
    rest = refs[n_a + 5:]
    h = x_ref[...]
    off = 0
    for a_ref in a_refs:
        ka = a_ref.shape[1]
        h = h + _dot(a_ref[...], wo_ref[off:off + ka, :])
        off += ka
    gate = jax.nn.sigmoid(_dot(h.astype(BF16), wg_ref[...]))
    h = h + gate * _dot(p_ref[...].astype(BF16), wp_ref[...])
    if final_norm:
        fg_ref, o_ref = rest
        o_ref[...] = _rms_norm(h, fg_ref[...])
    else:
        (o_ref,) = rest
        o_ref[...] = h


def _out_ple(a_list, x, p, w_out_bf16, w_gate_bf16, w_proj_bf16, final_g, tm):
    m, d = x.shape
    pd = p.shape[1]
    row = lambda i: (i, 0)
    fixed = lambda i: (0, 0)
    in_specs = [pl.BlockSpec((tm, a.shape[1]), row) for a in a_list]
    in_specs += [pl.BlockSpec((tm, d), row), pl.BlockSpec((tm, pd), row),
                 _resident(w_out_bf16.shape, fixed), _resident((d, d), fixed), _resident((pd, d), fixed)]
    args = list(a_list) + [x, p, w_out_bf16, w_gate_bf16, w_proj_bf16]
    if final_g is not None:
        in_specs.append(pl.BlockSpec((1, d), fixed))
        args.append(final_g.reshape(1, d))
    vmem = (w_out_bf16.size + d * d + pd * d) * 2 + 2 * tm * (2 * d * 4 + pd * 4 + w_out_bf16.shape[0] * 2) \
        + 4 * tm * d * 4 + (6 << 20)
    return pl.pallas_call(
        functools.partial(_out_ple_body, n_a=len(a_list), final_norm=final_g is not None),
        grid=(m // tm,),
        in_specs=in_specs,
        out_specs=pl.BlockSpec((tm, d), row),
        out_shape=jax.ShapeDtypeStruct((m, d), F32),
        compiler_params=_params(("parallel",), vmem),
        name="out_ple",
    )(*args)


def _shifted_inputs(h_ref, prev_ref, g_ref, *, seq_rows, prev_given):
    xn = _rms_norm(h_ref[...], g_ref[...])
    if prev_given:
        return xn, prev_ref[...]
    tm = xn.shape[0]
    before = _rms_norm(prev_ref[...], g_ref[...])[SUBLANES - 1:SUBLANES, :]
    starts_sequence = (pl.program_id(0) * tm) % seq_rows == 0
    before = jnp.where(starts_sequence, 0.0, before)
    first = lax.broadcasted_iota(jnp.int32, xn.shape, 0) == 0
    return xn, jnp.where(first, before, pltpu.roll(xn, 1, 0))


def _mix_mm_body(h_ref, prev_ref, g_ref, mu_ref, w_ref, o_ref, *, seq_rows, prev_given, act):
    xn, x_prev = _shifted_inputs(h_ref, prev_ref, g_ref, seq_rows=seq_rows, prev_given=prev_given)
    xm = xn + (x_prev - xn) * mu_ref[...]
    z = _dot(xm.astype(BF16), w_ref[...])
    o_ref[...] = act(z) if act is not None else z


def _lora_body(h_ref, prev_ref, g_ref, muw_ref, mua_ref, w1_ref, w2_ref, w0_ref, a1_ref, a2_ref, a0_ref,
               lw_ref, a_ref, *, seq_rows, prev_given):
    xn, x_prev = _shifted_inputs(h_ref, prev_ref, g_ref, seq_rows=seq_rows, prev_given=prev_given)
    xx = x_prev - xn
    xw = (xn + xx * muw_ref[...]).astype(BF16)
    xa = (xn + xx * mua_ref[...]).astype(BF16)
    zw = w0_ref[...] + _dot(jnp.tanh(_dot(xw, w1_ref[...])).astype(BF16), w2_ref[...])
    w_log = -jax.nn.softplus(-zw) - 0.5
    lw_ref[...] = -jnp.exp(w_log)
    za = a0_ref[...] + _dot(_dot(xa, a1_ref[...]).astype(BF16), a2_ref[...])
    a_ref[...] = jax.nn.sigmoid(za)


def _shift_specs(h, prev, tm, prev_given):
    m, d = h.shape
    row = lambda i: (i, 0)
    if prev_given:
        return [pl.BlockSpec((tm, d), row), pl.BlockSpec((tm, d), row)], [h, prev]
    per = tm // SUBLANES
    return ([pl.BlockSpec((tm, d), row), pl.BlockSpec((SUBLANES, d), lambda i: (jnp.maximum(i * per - 1, 0), 0))],
            [h, h])


def _mix_mm(h, prev, norm_g, mu_row, w_bf16, act, tm, seq_rows):
    m, d = h.shape
    n = w_bf16.shape[1]
    prev_given = prev is not None
    specs, args = _shift_specs(h, prev, tm, prev_given)
    fixed = lambda i: (0, 0)
    vmem = d * n * 2 + 4 * tm * d * 4 + 2 * tm * n * 4 + 4 * tm * d * 4 + (6 << 20)
    return pl.pallas_call(
        functools.partial(_mix_mm_body, seq_rows=seq_rows, prev_given=prev_given, act=act),
        grid=(m // tm,),
        in_specs=specs + [pl.BlockSpec((1, d), fixed), pl.BlockSpec((1, d), fixed), _resident((d, n), fixed)],
        out_specs=pl.BlockSpec((tm, n), lambda i: (i, 0)),
        out_shape=jax.ShapeDtypeStruct((m, n), F32),
        compiler_params=_params(("parallel",), vmem),
        name="rwkv_mix_mm",
    )(*args, norm_g.reshape(1, d), mu_row.reshape(1, d), w_bf16)


def _pad_lora(w_down, w_up):
    rank = w_down.shape[1]
    pad = (-rank) % LANES
    return (jnp.pad(w_down, ((0, 0), (0, pad))).astype(BF16), jnp.pad(w_up, ((0, pad), (0, 0))).astype(BF16))


def _lora(h, prev, norm_g, mu_w, mu_a, w1, w2, w0, a1, a2, a0, tm, seq_rows):
    m, d = h.shape
    prev_given = prev is not None
    specs, args = _shift_specs(h, prev, tm, prev_given)
    w1p, w2p = _pad_lora(w1, w2)
    a1p, a2p = _pad_lora(a1, a2)
    rank = w1p.shape[1]
    fixed = lambda i: (0, 0)
    vec = pl.BlockSpec((1, d), fixed)
    down = pl.BlockSpec((d, rank), fixed)
    up = pl.BlockSpec((rank, d), fixed)
    out = jax.ShapeDtypeStruct((m, d), F32)
    vmem = (2 + 4 + 10) * tm * d * 4 + 8 * d * rank * 2 + (4 << 20)
    return pl.pallas_call(
        functools.partial(_lora_body, seq_rows=seq_rows, prev_given=prev_given),
        grid=(m // tm,),
        in_specs=specs + [vec, vec, vec, down, up, vec, down, up, vec],
        out_specs=[pl.BlockSpec((tm, d), lambda i: (i, 0))] * 2,
        out_shape=[out, out],
        compiler_params=_params(("parallel",), vmem),
        name="rwkv_lora",
    )(*args, norm_g.reshape(1, d), mu_w.reshape(1, d), mu_a.reshape(1, d),
      w1p, w2p, w0.reshape(1, d), a1p, a2p, a0.reshape(1, d))


def _rwkv_prompt_body(r_ref, k_ref, v_ref, lw_ref, a_ref, g_ref, kk_ref, ka_ref, rk_ref, gng_ref, gnb_ref,
                      o_ref, st_ref, q2_scr, y0_scr, m_scr, h0_scr, hs_scr, *, seq, group):
    c_len = RWKV_CHUNK
    n_chunks = seq // c_len
    hd = C_HEAD_DIM
    assert c_len == hd and LANES % hd == 0, "time x time and key x value blocks share one lane tiling"
    lane = lax.broadcasted_iota(jnp.int32, (1, LANES), 1)
    head_masks = (lane < hd, lane >= hd)
    ti = lax.broadcasted_iota(jnp.int32, (c_len, c_len), 0)
    si = lax.broadcasted_iota(jnp.int32, (c_len, c_len), 1)
    tri = (ti >= si).astype(F32)
    tp = lax.broadcasted_iota(jnp.int32, (c_len, LANES), 0)
    sp = lax.broadcasted_iota(jnp.int32, (c_len, LANES), 1) % hd
    incl = tp >= sp
    strict = tp > sp
    eye_c = (tp == sp).astype(F32)
    bi = lax.broadcasted_iota(jnp.int32, (LANES, LANES), 0)
    bj = lax.broadcasted_iota(jnp.int32, (LANES, LANES), 1)
    same_head = (bi // hd) == (bj // hd)
    head_ones = same_head.astype(BF16)

    def head_sum(x):
        return _dot(x, head_ones, (2, 1))

    def stacked(z):
        masks = [jnp.concatenate([mk] * (z.shape[1] // LANES), axis=1) for mk in head_masks]
        return jnp.concatenate([jnp.where(mk, z, 0.0) for mk in masks], axis=0)

    def block_diag(x):
        return jnp.where(same_head, jnp.concatenate([x] * len(head_masks), axis=0), 0.0)

    def diag_blocks(x):
        x = jnp.where(same_head, x, 0.0)
        return x[:hd] + x[hd:]

    def load(c):
        rows = pl.ds(pl.multiple_of(c * c_len, c_len), c_len)
        r, k, v, a = r_ref[rows, :], k_ref[rows, :], v_ref[rows, :], a_ref[rows, :]
        k2 = k * (1.0 + (a - 1.0) * ka_ref[...])
        return rows, r, k, k2, v, a

    def prepare_group(gi, carry):
        chunks = [gi * group + u for u in range(group)]
        rows, r, k, k2, v, a = zip(*[load(c) for c in chunks])
        lw = [lw_ref[rw, :] for rw in rows]
        kk = [x * kk_ref[...] for x in k]
        norm = [head_sum(x * x) for x in kk]
        cum = [_dot(tri, x, (1, 3)) for x in lw]
        kk = [x / jnp.maximum(jnp.sqrt(n), 1e-12) for x, n in zip(kk, norm)]
        last = [x[c_len - 1:c_len, :] for x in cum]
        b = [x * y for x, y in zip(kk, a)]
        ekk = [x * jnp.exp(cm - w) for x, cm, w in zip(kk, cum, lw)]
        er = [x * jnp.exp(cm) for x, cm in zip(r, cum)]
        inv = [jnp.exp(-cm) for cm in cum]
        eb = [x * y for x, y in zip(b, inv)]
        ek = [x * y for x, y in zip(k2, inv)]
        to_end = [jnp.exp(ls - cm) for ls, cm in zip(last, cum)]
        eb_end = [x * y for x, y in zip(b, to_end)]
        ek_end = [x * y for x, y in zip(k2, to_end)]
        lhs = [jnp.concatenate([x, y], axis=0) for x, y in zip(ekk, er)]
        g = [_dot_nt(x, jnp.concatenate([stacked(y), stacked(z)], axis=0)) for x, y, z in zip(lhs, eb, ek)]
        l_b = [jnp.where(strict, x[:c_len, :LANES], 0.0) for x in g]
        a_b = [jnp.where(incl, x[c_len:, :LANES], 0.0) for x in g]
        l_k = [jnp.where(strict, x[:c_len, LANES:], 0.0) for x in g]
        a_k = [jnp.where(incl, x[c_len:, LANES:], 0.0) for x in g]
        lakv = [_dot(jnp.concatenate([x, y], axis=0), stacked(z)) for x, y, z in zip(l_k, a_k, v)]
        lkv = [x[:c_len] for x in lakv]
        akv = [x[c_len:] for x in lakv]
        levels = int(math.log2(c_len))
        power = [-x for x in l_b]
        inv_t = [eye_c + x for x in power]
        power = [_dot(x, block_diag(x)) for x in power]
        for level in range(1, levels):
            if level < levels - 1:
                both = [_dot(jnp.concatenate([x, y], axis=0), block_diag(x)) for x, y in zip(power, inv_t)]
                power = [x[:c_len] for x in both]
                inv_t = [x + y[c_len:] for x, y in zip(inv_t, both)]
            else:
                inv_t = [y + _dot(y, block_diag(x)) for x, y in zip(power, inv_t)]
        tz = [_dot(x, stacked(jnp.concatenate([y, z], axis=1))) for x, y, z in zip(inv_t, ekk, lkv)]
        qu = [jnp.concatenate([x[:, :LANES], -x[:, LANES:]], axis=1) for x in tz]
        ab = [_dot(x, stacked(y)) for x, y in zip(a_b, qu)]
        ends = [jnp.concatenate([x, y], axis=0) for x, y in zip(eb_end, ek_end)]
        tails = [jnp.concatenate([x, jnp.concatenate([jnp.zeros_like(y), y], axis=1)], axis=0) for x, y in zip(qu, v)]
        bq = [_dot_tn(x, y) for x, y in zip(ends, tails)]
        for u, c in enumerate(chunks):
            q2_scr[c] = er[u] - ab[u][:, :LANES]
            y0_scr[c] = ab[u][:, LANES:] + akv[u]
            m_scr[c] = eye_c * jnp.exp(last[u]) - diag_blocks(bq[u][:, :LANES])
            h0_scr[c] = diag_blocks(bq[u][:, LANES:])
        return carry

    lax.fori_loop(0, n_chunks // group, prepare_group, 0)

    def carry_state(c, state):
        hs_scr[c] = state
        return _dot(m_scr[c], stacked(state), (2, 2)) + h0_scr[c]

    state = lax.fori_loop(0, n_chunks, carry_state, jnp.zeros((hd, LANES), F32))
    st_ref[...] = _dot_tn(state, (ti == si).astype(F32), (3, 1))

    def emit_group(gi, carry):
        chunks = [gi * group + u for u in range(group)]
        rows, r, _, k2, v, _ = zip(*[load(c) for c in chunks])
        y = [_dot(q2_scr[c], stacked(hs_scr[c]), (2, 2)) + y0_scr[c] for c in chunks]
        bonus = [head_sum(x * y2 * rk_ref[...]) for x, y2 in zip(r, k2)]
        mean = [head_sum(x) * (1.0 / hd) for x in y]
        dev = [x - mu for x, mu in zip(y, mean)]
        var = [head_sum(x * x) * (1.0 / hd) for x in dev]
        for rw, dv, vr, bn, vv in zip(rows, dev, var, bonus, v):
            yn = dv * lax.rsqrt(vr + GN_EPS) * gng_ref[...] + gnb_ref[...]
            o_ref[rw, :] = ((yn + bn * vv) * g_ref[rw, :]).astype(o_ref.dtype)
        return carry

    lax.fori_loop(0, n_chunks // group, emit_group, 0)


def _rwkv_prompt(r, k, v, lw, a, g, k_k, k_a, r_k, gn_g, gn_b, batch, seq, group=32):
    d = r.shape[1]
    pairs = d // LANES
    n_chunks = seq // RWKV_CHUNK
    blk = pl.BlockSpec((None, seq, LANES), lambda b, p: (b, 0, p))
    vec = pl.BlockSpec((1, LANES), lambda b, p: (0, p))
    r3 = lambda t: t.reshape(batch, seq, d)
    v2 = lambda t: t.reshape(1, d)
    hd = C_HEAD_DIM
    vmem = 2 * 6 * seq * LANES * 4 + 2 * seq * LANES * 2 + n_chunks * 5 * RWKV_CHUNK * LANES * 4 + (24 << 20)
    chunk_scratch = pltpu.VMEM((n_chunks, RWKV_CHUNK, LANES), F32)
    out, state = pl.pallas_call(
        functools.partial(_rwkv_prompt_body, seq=seq, group=group),
        grid=(batch, pairs),
        in_specs=[blk] * 6 + [vec] * 5,
        out_specs=[blk, pl.BlockSpec((None, None, LANES, hd), lambda b, p: (b, p, 0, 0))],
        out_shape=[jax.ShapeDtypeStruct((batch, seq, d), BF16),
                   jax.ShapeDtypeStruct((batch, pairs, LANES, hd), F32)],
        scratch_shapes=[chunk_scratch] * 5,
        compiler_params=_params(("parallel", "parallel"), vmem),
        name="rwkv_prompt",
    )(r3(r), r3(k), r3(v), r3(lw), r3(a), r3(g), v2(k_k), v2(k_a), v2(r_k), v2(gn_g), v2(gn_b))
    return out.reshape(batch * seq, d), state.reshape(batch, d // hd, hd, hd)


def _rwkv_sample_body(s_ref, r_ref, k_ref, lw_ref, a_ref, v_ref, kk_ref, ka_ref, rk_ref, gng_ref, gnb_ref,
                      y_ref, so_ref):
    r, k, a = r_ref[...], k_ref[...], a_ref[...]
    v = v_ref[...]
    kk = k * kk_ref[...]
    kk = kk / jnp.maximum(jnp.sqrt(jnp.sum(kk * kk, axis=-1, keepdims=True)), 1e-12)
    k2 = k * (1.0 + (a - 1.0) * ka_ref[...])
    s = s_ref[...]
    sa = -jnp.sum(s * kk, axis=-1, keepdims=True)
    s = s * jnp.exp(lw_ref[...]) + sa * (kk * a) + v * k2
    so_ref[...] = s
    y = jnp.sum(s * r, axis=-1, keepdims=True)
    mean = jnp.mean(y, axis=1, keepdims=True)
    var = jnp.mean(jnp.square(y - mean), axis=1, keepdims=True)
    y = (y - mean) * lax.rsqrt(var + GN_EPS) * gng_ref[...] + gnb_ref[...]
    bonus = jnp.sum(r * k2 * rk_ref[...], axis=-1, keepdims=True)
    y_ref[...] = y + bonus * v


def _rwkv_sample(state, r, k, v, lw, a, k_k, k_a, r_k, gn_g, gn_b):
    nb, heads, hd, _ = state.shape
    rowv = lambda t: t.reshape(nb, heads, 1, hd)
    rowp = lambda t: t.reshape(heads, 1, hd)
    colp = lambda t: t.reshape(heads, hd, 1)
    st = pl.BlockSpec((None, heads, hd, hd), lambda b: (b, 0, 0, 0))
    rv = pl.BlockSpec((None, heads, 1, hd), lambda b: (b, 0, 0, 0))
    cv = pl.BlockSpec((None, heads, hd, 1), lambda b: (b, 0, 0, 0))
    rp = pl.BlockSpec((heads, 1, hd), lambda b: (0, 0, 0))
    cp = pl.BlockSpec((heads, hd, 1), lambda b: (0, 0, 0))
    y, new_state = pl.pallas_call(
        _rwkv_sample_body,
        grid=(nb,),
        in_specs=[st, rv, rv, rv, rv, cv, rp, rp, rp, cp, cp],
        out_specs=[cv, st],
        out_shape=[jax.ShapeDtypeStruct((nb, heads, hd, 1), F32), jax.ShapeDtypeStruct(state.shape, F32)],
        compiler_params=_params(("parallel",), 32 << 20),
        name="rwkv_sample",
    )(state, rowv(r), rowv(k), rowv(lw), rowv(a), v.reshape(nb, heads, hd, 1),
      rowp(k_k), rowp(k_a), rowp(r_k), colp(gn_g), colp(gn_b))
    return y.reshape(nb, heads * hd), new_state


def _gate_cast_body(y_ref, g_ref, o_ref):
    o_ref[...] = (y_ref[...] * g_ref[...]).astype(o_ref.dtype)


def _gate_cast(y, g):
    return pl.pallas_call(_gate_cast_body, out_shape=jax.ShapeDtypeStruct(y.shape, BF16), name="gate_cast")(y, g)


def kernel(x_prompt, x_sample, cache_a_k, cache_a_v, state_c_wkv, state_c_shift, p_prompt, p_sample, norm_g, final_norm_g, rel_bias, ab_w_in, ab_w_out, b_w_s, b_b_s, b_ln_g, b_ln_b, c_mu, c_w_r, c_w_k, c_w_v, c_w_g, c_w_o, c_w0, c_w1, c_w2, c_a0, c_a1, c_a2, c_k_k, c_k_a, c_r_k, c_gn_g, c_gn_b, ple_w_proj, ple_w_gate):
    batch, seq, d = x_prompt.shape
    nb = x_sample.shape[0]
    assert x_sample.shape[1] == 1 and seq % (QBLK * max(dil for _, dil in DILATION_PATTERNS)) == 0
    assert norm_g.shape[0] == 2, "layer pattern implemented for depth 2: one attention+gMLP layer, one RWKV-7 layer"
    a_heads = cache_a_k.shape[3]
    c_heads = state_c_wkv.shape[2]
    m = batch * seq
    hp = x_prompt.reshape(m, d)
    hs = x_sample.reshape(nb, d)
    pp = p_prompt.reshape(p_prompt.shape[0], m, -1)
    ps = p_sample.reshape(p_sample.shape[0], nb, -1)
    w_gate = ple_w_gate.astype(BF16)
    w_proj = ple_w_proj.astype(BF16)

    w_in = ab_w_in[0].astype(BF16)
    w_out = ab_w_out[0].astype(BF16)
    bias_prompt = _prompt_bias(rel_bias)
    bias_cache, bias_own = _sample_bias(rel_bias)

    q, k, v, ga, ub, vn, gb = _ab_in_proj(hp, norm_g[0], w_in, b_ln_g[0], b_ln_b[0], tm=512)
    oa = _attn_prompt(q, k, v, ga, bias_prompt, batch, seq)
    ob = _gmlp_prompt(vn, ub, gb, b_w_s[0], b_b_s[0], tm=512)
    hp = _out_ple([oa, ob], hp, pp[0], w_out, w_gate[0], w_proj[0], None, tm=256)
    a_k_p = k.reshape(1, batch, seq, a_heads, A_HEAD_DIM)
    a_v_p = v.reshape(1, batch, seq, a_heads, A_HEAD_DIM)

    qs, ks, vs, gas, ubs, vns, gbs = _ab_in_proj(hs, norm_g[0], w_in, b_ln_g[0], b_ln_b[0], tm=nb)
    oas = _attn_sample(qs, ks, vs, gas, cache_a_k[0], cache_a_v[0], bias_cache, bias_own)
    obs = _gmlp_sample(vns, ubs, gbs, b_w_s[0], b_b_s[0])
    hs = _out_ple([oas, obs], hs, ps[0], w_out, w_gate[0], w_proj[0], None, tm=nb)
    a_k_s = ks.reshape(1, nb, 1, a_heads, A_HEAD_DIM)
    a_v_s = vs.reshape(1, nb, 1, a_heads, A_HEAD_DIM)
    b_v_s = vns.reshape(1, nb, 1, -1)

    mu = c_mu[0]
    w_r, w_k, w_v, w_g, w_o = (t[0].astype(BF16) for t in (c_w_r, c_w_k, c_w_v, c_w_g, c_w_o))
    rk_flat = c_r_k[0].reshape(-1)

    def projections(h, prev, tm, seq_rows):
        r = _mix_mm(h, prev, norm_g[1], mu[0], w_r, None, tm, seq_rows)
        kx = _mix_mm(h, prev, norm_g[1], mu[2], w_k, None, tm, seq_rows)
        vx = _mix_mm(h, prev, norm_g[1], mu[3], w_v, None, tm, seq_rows)
        g = _mix_mm(h, prev, norm_g[1], mu[5], w_g, _silu, tm, seq_rows)
        lw, a = _lora(h, prev, norm_g[1], mu[1], mu[4], c_w1[0], c_w2[0], c_w0[0], c_a1[0], c_a2[0], c_a0[0],
                      min(tm, 256), seq_rows)
        return r, kx, vx, g, lw, a

    r, kx, vx, g, lw, a = projections(hp, None, 512, seq)
    yg, s_p = _rwkv_prompt(r, kx, vx, lw, a, g, c_k_k[0], c_k_a[0], rk_flat, c_gn_g[0], c_gn_b[0], batch, seq)
    y_prompt = _out_ple([yg], hp, pp[1], w_o, w_gate[1], w_proj[1], final_norm_g, tm=256)
    last_rows = hp.reshape(batch, seq, d)[:, seq - SUBLANES:, :].reshape(batch * SUBLANES, d)
    sh_p = _norm_rows(last_rows, norm_g[1]).reshape(batch, SUBLANES, d)[:, SUBLANES - 1]

    rs, kxs, vxs, gs, lws, a_s = projections(hs, state_c_shift[0], nb, 1)
    ys, s_s = _rwkv_sample(state_c_wkv[0], rs, kxs, vxs, lws, a_s, c_k_k[0], c_k_a[0], rk_flat, c_gn_g[0], c_gn_b[0])
    y_sample = _out_ple([_gate_cast(ys, gs)], hs, ps[1], w_o, w_gate[1], w_proj[1], final_norm_g, tm=nb)
    sh_s = _norm_rows(hs, norm_g[1])

    return (y_prompt.reshape(batch, seq, d), y_sample.reshape(nb, 1, d), a_k_p, a_v_p, a_k_s, a_v_s, b_v_s,
            s_p[None], sh_p[None], s_s[None], sh_s[None])


def _norm_rows_body(x_ref, g_ref, o_ref):
    o_ref[...] = _rms_norm(x_ref[...], g_ref[...])


def _norm_rows(x, g):
    return pl.pallas_call(_norm_rows_body, out_shape=jax.ShapeDtypeStruct(x.shape, F32), name="norm_rows")(
        x, g.reshape(1, -1))
```

```python
import functools
import math

import jax
import jax.numpy as jnp
from jax import lax
from jax.experimental import pallas as pl
from jax.experimental.pallas import tpu as pltpu

F32 = jnp.float32
BF16 = jnp.bfloat16

LANES = 128
SUBLANES = 8
VMEM_BUDGET_BYTES = 56 * 1024 * 1024

A_HEAD_DIM = 128
DILATION_PATTERNS = ((128, 1), (512, 4), (2048, 16))
QBLK = 128
ATTN_GROUP = 8
REL_BUCKETS = 32
REL_MAX_DIST = 2048
CHUNK = 128
C_HEAD_DIM = 64
CARRY_EVERY = 12
RWKV_CHUNK = 64
RMS_EPS = 1e-6
LN_EPS = 1e-5
GN_EPS = 64e-5
NEG_INF = -1e30


def _params(semantics, vmem_bytes):
    return pltpu.CompilerParams(dimension_semantics=semantics, vmem_limit_bytes=int(vmem_bytes))


def _resident(shape, index_map):
    return pl.BlockSpec(shape, index_map, pipeline_mode=pl.Buffered(1))


def _bf16_terms(x, n):
    if x.dtype == BF16 or n == 1:
        return [x.astype(BF16)]
    terms, rest = [], x
    for _ in range(n):
        terms.append(rest.astype(BF16))
        rest = rest - terms[-1].astype(F32)
    return terms


def _dot_dims(a, b, dims, terms):
    a_terms = _bf16_terms(a, terms[0])
    b_terms = _bf16_terms(b, terms[1])
    out = None
    for i, at in enumerate(a_terms):
        for j, bt in enumerate(b_terms):
            if i + j < max(len(a_terms), len(b_terms)):
                part = lax.dot_general(at, bt, (dims, ((), ())), preferred_element_type=F32)
                out = part if out is None else out + part
    return out


def _dot(a, b, terms=(1, 1)):
    return _dot_dims(a, b, ((1,), (0,)), terms)


def _dot_nt(a, b, terms=(1, 1)):
    return _dot_dims(a, b, ((1,), (1,)), terms)


def _dot_tn(a, b, terms=(1, 1)):
    return _dot_dims(a, b, ((0,), (0,)), terms)


def _rms_norm(x, g):
    return x * lax.rsqrt(jnp.mean(x * x, axis=-1, keepdims=True) + RMS_EPS) * g


def _layer_norm(x, g, b):
    mu = jnp.mean(x, axis=-1, keepdims=True)
    var = jnp.mean(jnp.square(x - mu), axis=-1, keepdims=True)
    return (x - mu) * lax.rsqrt(var + LN_EPS) * g + b


def _silu(x):
    return x * jax.nn.sigmoid(x)


def _ab_in_body(x_ref, g_ref, w_ref, lng_ref, lnb_ref,
                q_ref, k_ref, v_ref, ga_ref, ub_ref, vn_ref, gb_ref):
    xn = _rms_norm(x_ref[...], g_ref[...]).astype(BF16)
    width = q_ref.shape[1]
    epilogues = (
        (q_ref, lambda t: t),
        (k_ref, lambda t: t),
        (v_ref, lambda t: t),
        (ga_ref, _silu),
        (ub_ref, jax.nn.gelu),
        (vn_ref, lambda t: _layer_norm(jax.nn.gelu(t), lng_ref[...], lnb_ref[...])),
        (gb_ref, _silu),
    )
    for idx, (ref, fn) in enumerate(epilogues):
        z = _dot(xn, w_ref[:, idx * width:(idx + 1) * width])
        ref[...] = fn(z).astype(ref.dtype)


def _ab_in_proj(x, norm_g, w_in_bf16, ln_g, ln_b, tm):
    m, d = x.shape
    width = w_in_bf16.shape[1] // 7
    out = jax.ShapeDtypeStruct((m, width), F32)
    row = lambda i: (i, 0)
    fixed = lambda i: (0, 0)
    vmem = w_in_bf16.size * 2 + 2 * tm * d * 4 + 7 * 2 * tm * width * 4 + tm * d * 2 + 6 * tm * width * 4
    return pl.pallas_call(
        _ab_in_body,
        grid=(m // tm,),
        in_specs=[
            pl.BlockSpec((tm, d), row),
            pl.BlockSpec((1, d), fixed),
            _resident(w_in_bf16.shape, fixed),
            pl.BlockSpec((1, width), fixed),
            pl.BlockSpec((1, width), fixed),
        ],
        out_specs=[pl.BlockSpec((tm, width), row)] * 7,
        out_shape=[out] * 7,
        compiler_params=_params(("parallel",), vmem),
        name="ab_in_proj",
    )(x, norm_g.reshape(1, d), w_in_bf16, ln_g.reshape(1, width), ln_b.reshape(1, width))


def _t5_bucket(dist):
    n_exact = REL_BUCKETS // 2
    d = jnp.maximum(dist, 1).astype(F32)
    log_b = n_exact + (jnp.log(d / n_exact) / math.log(REL_MAX_DIST / n_exact) * (REL_BUCKETS - n_exact)).astype(jnp.int32)
    return jnp.where(dist < n_exact, dist, jnp.minimum(log_b, REL_BUCKETS - 1))


def _bias_at(rel_bias, dist):
    one_hot = jax.nn.one_hot(_t5_bucket(dist), REL_BUCKETS, dtype=F32)
    return jnp.einsum("...k,kh->...h", one_hot, rel_bias.astype(F32), precision=lax.Precision.HIGHEST)


def _prompt_bias(rel_bias):
    i = jnp.arange(QBLK)[:, None]
    j = jnp.arange(2 * QBLK)[None, :]
    steps = QBLK + i - j
    tables = []
    for window, dil in DILATION_PATTERNS:
        band = (steps >= 0) & (steps <= window // dil)
        bias = jnp.moveaxis(_bias_at(rel_bias, jnp.clip(steps, 0) * dil), -1, 0)
        tables.append(jnp.where(band[None], bias, NEG_INF))
    return jnp.stack(tables)


def _sample_bias(rel_bias):
    back = QBLK - jnp.arange(QBLK)
    cache = jnp.stack([_bias_at(rel_bias, back * dil) for _, dil in DILATION_PATTERNS])
    own = _bias_at(rel_bias, jnp.zeros((), jnp.int32))
    return cache, own


def _attn_prompt_body(q_ref, k_ref, v_ref, gate_ref, bias_ref, o_ref, o_scr, lse_scr, *, seq):
    scale = A_HEAD_DIM ** -0.5

    def blocks(p, dil, starts, with_prev):
        ds = lambda st: pl.ds(st, QBLK, stride=dil) if dil > 1 else pl.ds(st, QBLK)
        rows = [ds(st) for st in starts]
        q = [q_ref[rw, :].astype(BF16) for rw in rows]
        k = [k_ref[rw, :] for rw in rows]
        v = [v_ref[rw, :] for rw in rows]
        if with_prev:
            prev = [ds(st - QBLK * dil) for st in starts]
            k = [jnp.concatenate([k_ref[pv, :], x], axis=0) for pv, x in zip(prev, k)]
            v = [jnp.concatenate([v_ref[pv, :], x], axis=0) for pv, x in zip(prev, v)]
            bias = bias_ref[p]
        else:
            bias = bias_ref[p, :, QBLK:]
        s = [_dot_nt(x, y) * scale + bias for x, y in zip(q, k)]
        m = [jnp.max(x, axis=-1, keepdims=True) for x in s]
        e = [jnp.exp(x - y) for x, y in zip(s, m)]
        l = [jnp.sum(x, axis=-1, keepdims=True) for x in e]
        o = [_dot(x, y) / z for x, y, z in zip(e, v, l)]
        for rw, x, y, z in zip(rows, o, m, l):
            o_scr[p, rw, :] = x
            lse_scr[p, rw, :] = jnp.broadcast_to(y + jnp.log(z), (QBLK, A_HEAD_DIM))

    def widest_group(count):
        return max(g for g in range(1, ATTN_GROUP + 1) if count % g == 0)

    for p, (_, dil) in enumerate(DILATION_PATTERNS):
        nb = seq // (QBLK * dil)
        shift = dil.bit_length() - 1
        g_first = widest_group(dil)

        def first(i, carry, p=p, dil=dil, g=g_first):
            blocks(p, dil, [i * g + u for u in range(g)], False)
            return carry

        lax.fori_loop(0, dil // g_first, first, 0)
        if nb > 1:
            g_rest = widest_group(dil * (nb - 1))

            def rest(i, carry, p=p, dil=dil, shift=shift, g=g_rest):
                idx = [i * g + u for u in range(g)]
                blocks(p, dil, [((x >> shift) + 1) * (QBLK * dil) + (x & (dil - 1)) for x in idx], True)
                return carry

            lax.fori_loop(0, dil * (nb - 1) // g_rest, rest, 0)

    lse = [lse_scr[p] for p in range(len(DILATION_PATTERNS))]
    m = functools.reduce(jnp.maximum, lse)
    e = [jnp.exp(t - m) for t in lse]
    den = functools.reduce(jnp.add, e)
    mix = functools.reduce(jnp.add, [(e[p] / den) * o_scr[p] for p in range(len(e))])
    o_ref[...] = (mix * gate_ref[...]).astype(o_ref.dtype)


def _attn_prompt(q, k, v, gate, bias, batch, seq):
    width = q.shape[1]
    heads = width // A_HEAD_DIM
    n_pat = len(DILATION_PATTERNS)
    blk = pl.BlockSpec((None, seq, A_HEAD_DIM), lambda b, h: (b, 0, h))
    r3 = lambda t: t.reshape(batch, seq, width)
    vmem = 4 * 2 * seq * A_HEAD_DIM * 4 + 2 * seq * A_HEAD_DIM * 2 + 2 * n_pat * seq * A_HEAD_DIM * 4 \
        + 2 * n_pat * QBLK * 2 * QBLK * 4 + (8 << 20)
    out = pl.pallas_call(
        functools.partial(_attn_prompt_body, seq=seq),
        grid=(batch, heads),
        in_specs=[blk, blk, blk, blk,
                  pl.BlockSpec((n_pat, None, QBLK, 2 * QBLK), lambda b, h: (0, h, 0, 0))],
        out_specs=blk,
        out_shape=jax.ShapeDtypeStruct((batch, seq, width), BF16),
        scratch_shapes=[pltpu.VMEM((n_pat, seq, A_HEAD_DIM), F32), pltpu.VMEM((n_pat, seq, A_HEAD_DIM), F32)],
        compiler_params=_params(("parallel", "parallel"), vmem),
        name="attn_prompt",
    )(r3(q), r3(k), r3(v), r3(gate), bias)
    return out.reshape(batch * seq, width)


def _attn_sample_body(q_ref, kn_ref, vn_ref, gate_ref, bc_ref, bo_ref, *rest):
    n_pat = len(DILATION_PATTERNS)
    kc_refs, vc_refs, o_ref = rest[:n_pat], rest[n_pat:2 * n_pat], rest[2 * n_pat]
    scale = A_HEAD_DIM ** -0.5
    rnd = lambda t: t.astype(BF16).astype(F32)
    q = rnd(q_ref[...])
    kn = rnd(kn_ref[...])
    vn = rnd(vn_ref[...])
    s_new = jnp.sum(q * kn, axis=-1, keepdims=True) * scale + bo_ref[...]
    outs, lses = [], []
    for p in range(n_pat):
        kc = rnd(kc_refs[p][...])
        vc = rnd(vc_refs[p][...])
        s = jnp.sum(q[None] * kc, axis=-1, keepdims=True) * scale + bc_ref[p]
        m = jnp.maximum(jnp.max(s, axis=0), s_new)
        e = jnp.exp(s - m[None])
        e_new = jnp.exp(s_new - m)
        l = jnp.sum(e, axis=0) + e_new
        outs.append((jnp.sum(rnd(e) * vc, axis=0) + rnd(e_new) * vn) / l)
        lses.append(m + jnp.log(l))
    m = functools.reduce(jnp.maximum, lses)
    e = [jnp.exp(t - m) for t in lses]
    den = functools.reduce(jnp.add, e)
    mix = functools.reduce(jnp.add, [(e[p] / den) * outs[p] for p in range(n_pat)])
    o_ref[...] = (mix * gate_ref[...]).astype(o_ref.dtype)


def _attn_sample(q, k_new, v_new, gate, cache_k, cache_v, bias_cache, bias_own):
    nb, past, heads, hd = cache_k.shape
    n_pat = len(DILATION_PATTERNS)
    row = pl.BlockSpec((None, heads, hd), lambda b: (b, 0, 0))
    r3 = lambda t: t.reshape(nb, heads, hd)
    cache_specs, cache_args = [], []
    for cache in (cache_k, cache_v):
        for window, dil in DILATION_PATTERNS:
            last = past // (QBLK * dil) - 1
            cache_specs.append(pl.BlockSpec((None, QBLK, None, heads, hd), lambda b, last=last: (b, last, 0, 0, 0)))
            cache_args.append(cache.reshape(nb, past // dil, dil, heads, hd))
    vmem = 2 * 2 * n_pat * QBLK * heads * hd * 4 + 8 * QBLK * heads * hd * 4 + (8 << 20)
    out = pl.pallas_call(
        _attn_sample_body,
        grid=(nb,),
        in_specs=[row, row, row, row,
                  pl.BlockSpec((n_pat, QBLK, heads, 1), lambda b: (0, 0, 0, 0)),
                  pl.BlockSpec((heads, 1), lambda b: (0, 0))] + cache_specs,
        out_specs=row,
        out_shape=jax.ShapeDtypeStruct((nb, heads, hd), BF16),
        compiler_params=_params(("parallel",), vmem),
        name="attn_sample",
    )(r3(q), r3(k_new), r3(v_new), r3(gate), bias_cache[..., None], bias_own.reshape(heads, 1), *cache_args)
    return out.reshape(nb, heads * hd)


def _gmlp_prompt_body(vn_ref, ub_ref, gb_ref, ws_ref, bs_ref, o_ref):
    tm = vn_ref.shape[0]
    groups = ws_ref.shape[0]
    ii = lax.broadcasted_iota(jnp.int32, (CHUNK, CHUNK), 0)
    jj = lax.broadcasted_iota(jnp.int32, (CHUNK, CHUNK), 1)
    for g in range(groups):
        w = jnp.where(jj <= ii, ws_ref[g], 0.0).astype(BF16)
        cols = slice(g * LANES, (g + 1) * LANES)
        for c in range(tm // CHUNK):
            rows = slice(c * CHUNK, (c + 1) * CHUNK)
            s = _dot(w, vn_ref[rows, cols].astype(BF16)) + bs_ref[:, cols]
            o_ref[rows, cols] = (ub_ref[rows, cols] * s * gb_ref[rows, cols]).astype(o_ref.dtype)


def _gmlp_prompt(vn, ub, gb, w_s, b_s, tm):
    m, width = vn.shape
    groups = w_s.shape[0]
    bs_rows = jnp.repeat(b_s.T, width // groups, axis=1)
    row = pl.BlockSpec((tm, width), lambda i: (i, 0))
    vmem = 3 * 2 * tm * width * 4 + 2 * tm * width * 2 + (8 << 20)
    return pl.pallas_call(
        _gmlp_prompt_body,
        grid=(m // tm,),
        in_specs=[row, row, row,
                  pl.BlockSpec((groups, CHUNK, CHUNK), lambda i: (0, 0, 0)),
                  pl.BlockSpec((CHUNK, width), lambda i: (0, 0))],
        out_specs=row,
        out_shape=jax.ShapeDtypeStruct((m, width), BF16),
        compiler_params=_params(("parallel",), vmem),
        name="gmlp_prompt",
    )(vn, ub, gb, w_s, bs_rows)


def _gmlp_sample_body(vn_ref, ub_ref, gb_ref, w0_ref, b0_ref, o_ref):
    s = w0_ref[...].astype(BF16).astype(F32) * vn_ref[...].astype(BF16).astype(F32) + b0_ref[...]
    o_ref[...] = (ub_ref[...] * s * gb_ref[...]).astype(o_ref.dtype)


def _gmlp_sample(vn, ub, gb, w_s, b_s):
    m, width = vn.shape
    groups = w_s.shape[0]
    w0 = jnp.repeat(w_s[:, 0, 0], width // groups).reshape(1, width)
    b0 = jnp.repeat(b_s[:, 0], width // groups).reshape(1, width)
    return pl.pallas_call(
        _gmlp_sample_body,
        out_shape=jax.ShapeDtypeStruct((m, width), BF16),
        name="gmlp_sample",
    )(vn, ub, gb, w0, b0)


def _out_ple_body(*refs, n_a, final_norm):
    a_refs = refs[:n_a]
    x_ref, p_ref, wo_ref, wg_ref, wp_ref = refs[n_a:n_a + 5]
    rest = refs[n_a + 5:]
    h = x_ref[...]
    off = 0
    for a_ref in a_refs:
        ka = a_ref.shape[1]
        h = h + _dot(a_ref[...], wo_ref[off:off + ka, :])
        off += ka
    gate = jax.nn.sigmoid(_dot(h.astype(BF16), wg_ref[...]))
    h = h + gate * _dot(p_ref[...].astype(BF16), wp_ref[...])
    if final_norm:
        fg_ref, o_ref = rest
        o_ref[...] = _rms_norm(h, fg_ref[...])
    else:
        (o_ref,) = rest
        o_ref[...] = h


def _out_ple(a_list, x, p, w_out_bf16, w_gate_bf16, w_proj_bf16, final_g, tm):
    m, d = x.shape
    pd = p.shape[1]
    row = lambda i: (i, 0)
    fixed = lambda i: (0, 0)
    in_specs = [pl.BlockSpec((tm, a.shape[1]), row) for a in a_list]
    in_specs += [pl.BlockSpec((tm, d), row), pl.BlockSpec((tm, pd), row),
                 _resident(w_out_bf16.shape, fixed), _resident((d, d), fixed), _resident((pd, d), fixed)]
    args = list(a_list) + [x, p, w_out_bf16, w_gate_bf16, w_proj_bf16]
    if final_g is not None:
        in_specs.append(pl.BlockSpec((1, d), fixed))
        args.append(final_g.reshape(1, d))
    vmem = (w_out_bf16.size + d * d + pd * d) * 2 + 2 * tm * (2 * d * 4 + pd * 4 + w_out_bf16.shape[0] * 2) \
        + 4 * tm * d * 4 + (6 << 20)
    return pl.pallas_call(
        functools.partial(_out_ple_body, n_a=len(a_list), final_norm=final_g is not None),
        grid=(m // tm,),
        in_specs=in_specs,
        out_specs=pl.BlockSpec((tm, d), row),
        out_shape=jax.ShapeDtypeStruct((m, d), F32),
        compiler_params=_params(("parallel",), vmem),
        name="out_ple",
    )(*args)


def _shifted_inputs(h_ref, prev_ref, g_ref, *, seq_rows, prev_given):
    xn = _rms_norm(h_ref[...], g_ref[...])
    if prev_given:
        return xn, prev_ref[...]
    tm = xn.shape[0]
    before = _rms_norm(prev_ref[...], g_ref[...])[SUBLANES - 1:SUBLANES, :]
    starts_sequence = (pl.program_id(0) * tm) % seq_rows == 0
    before = jnp.where(starts_sequence, 0.0, before)
    first = lax.broadcasted_iota(jnp.int32, xn.shape, 0) == 0
    return xn, jnp.where(first, before, pltpu.roll(xn, 1, 0))


def _mix_mm_body(h_ref, prev_ref, g_ref, mu_ref, w_ref, o_ref, *, seq_rows, prev_given, act):
    xn, x_prev = _shifted_inputs(h_ref, prev_ref, g_ref, seq_rows=seq_rows, prev_given=prev_given)
    xm = xn + (x_prev - xn) * mu_ref[...]
    z = _dot(xm.astype(BF16), w_ref[...])
    o_ref[...] = act(z) if act is not None else z


def _lora_body(h_ref, prev_ref, g_ref, muw_ref, mua_ref, w1_ref, w2_ref, w0_ref, a1_ref, a2_ref, a0_ref,
               lw_ref, a_ref, *, seq_rows, prev_given):
    xn, x_prev = _shifted_inputs(h_ref, prev_ref, g_ref, seq_rows=seq_rows, prev_given=prev_given)
    xx = x_prev - xn
    xw = (xn + xx * muw_ref[...]).astype(BF16)
    xa = (xn + xx * mua_ref[...]).astype(BF16)
    zw = w0_ref[...] + _dot(jnp.tanh(_dot(xw, w1_ref[...])).astype(BF16), w2_ref[...])
    lw_ref[...] = -math.exp(-0.5) * jax.nn.sigmoid(zw)
    za = a0_ref[...] + _dot(_dot(xa, a1_ref[...]).astype(BF16), a2_ref[...])
    a_ref[...] = jax.nn.sigmoid(za)


def _shift_specs(h, prev, tm, prev_given):
    m, d = h.shape
    row = lambda i: (i, 0)
    if prev_given:
        return [pl.BlockSpec((tm, d), row), pl.BlockSpec((tm, d), row)], [h, prev]
    per = tm // SUBLANES
    return ([pl.BlockSpec((tm, d), row), pl.BlockSpec((SUBLANES, d), lambda i: (jnp.maximum(i * per - 1, 0), 0))],
            [h, h])


def _mix_mm(h, prev, norm_g, mu_row, w_bf16, act, tm, seq_rows):
    m, d = h.shape
    n = w_bf16.shape[1]
    prev_given = prev is not None
    specs, args = _shift_specs(h, prev, tm, prev_given)
    fixed = lambda i: (0, 0)
    vmem = d * n * 2 + 4 * tm * d * 4 + 2 * tm * n * 4 + 4 * tm * d * 4 + (6 << 20)
    return pl.pallas_call(
        functools.partial(_mix_mm_body, seq_rows=seq_rows, prev_given=prev_given, act=act),
        grid=(m // tm,),
        in_specs=specs + [pl.BlockSpec((1, d), fixed), pl.BlockSpec((1, d), fixed), _resident((d, n), fixed)],
        out_specs=pl.BlockSpec((tm, n), lambda i: (i, 0)),
        out_shape=jax.ShapeDtypeStruct((m, n), F32),
        compiler_params=_params(("parallel",), vmem),
        name="rwkv_mix_mm",
    )(*args, norm_g.reshape(1, d), mu_row.reshape(1, d), w_bf16)


def _pad_lora(w_down, w_up):
    rank = w_down.shape[1]
    pad = (-rank) % LANES
    return (jnp.pad(w_down, ((0, 0), (0, pad))).astype(BF16), jnp.pad(w_up, ((0, pad), (0, 0))).astype(BF16))


def _lora(h, prev, norm_g, mu_w, mu_a, w1, w2, w0, a1, a2, a0, tm, seq_rows):
    m, d = h.shape
    prev_given = prev is not None
    specs, args = _shift_specs(h, prev, tm, prev_given)
    w1p, w2p = _pad_lora(w1, w2)
    a1p, a2p = _pad_lora(a1, a2)
    rank = w1p.shape[1]
    fixed = lambda i: (0, 0)
    vec = pl.BlockSpec((1, d), fixed)
    down = pl.BlockSpec((d, rank), fixed)
    up = pl.BlockSpec((rank, d), fixed)
    out = jax.ShapeDtypeStruct((m, d), F32)
    vmem = (2 + 4 + 10) * tm * d * 4 + 8 * d * rank * 2 + (4 << 20)
    return pl.pallas_call(
        functools.partial(_lora_body, seq_rows=seq_rows, prev_given=prev_given),
        grid=(m // tm,),
        in_specs=specs + [vec, vec, vec, down, up, vec, down, up, vec],
        out_specs=[pl.BlockSpec((tm, d), lambda i: (i, 0))] * 2,
        out_shape=[out, out],
        compiler_params=_params(("parallel",), vmem),
        name="rwkv_lora",
    )(*args, norm_g.reshape(1, d), mu_w.reshape(1, d), mu_a.reshape(1, d),
      w1p, w2p, w0.reshape(1, d), a1p, a2p, a0.reshape(1, d))


def _rwkv_prompt_body(r_ref, k_ref, v_ref, lw_ref, a_ref, g_ref, kk_ref, ka_ref, rk_ref, gng_ref, gnb_ref,
                      o_ref, st_ref, q2_scr, y0_scr, m_scr, h0_scr, bv_scr, gate_scr, hs_scr, *, seq):
    c_len = RWKV_CHUNK
    n_chunks = seq // c_len
    hd = C_HEAD_DIM
    assert c_len == hd and LANES % hd == 0, "time x time and key x value blocks share one lane tiling"
    lane = lax.broadcasted_iota(jnp.int32, (1, LANES), 1)
    head_masks = (lane < hd, lane >= hd)
    ti = lax.broadcasted_iota(jnp.int32, (c_len, c_len), 0)
    si = lax.broadcasted_iota(jnp.int32, (c_len, c_len), 1)
    tri = (ti >= si).astype(F32)
    tp = lax.broadcasted_iota(jnp.int32, (c_len, LANES), 0)
    sp = lax.broadcasted_iota(jnp.int32, (c_len, LANES), 1) % hd
    incl = tp >= sp
    strict = tp > sp
    eye_c = (tp == sp).astype(F32)
    bi = lax.broadcasted_iota(jnp.int32, (LANES, LANES), 0)
    bj = lax.broadcasted_iota(jnp.int32, (LANES, LANES), 1)
    same_head = (bi // hd) == (bj // hd)
    head_ones = same_head.astype(BF16)

    def head_sum(x):
        return _dot(x, head_ones, (2, 1))

    def stacked(z):
        masks = [jnp.concatenate([mk] * (z.shape[1] // LANES), axis=1) for mk in head_masks]
        return jnp.concatenate([jnp.where(mk, z, 0.0) for mk in masks], axis=0)

    def block_diag(x):
        return jnp.where(same_head, jnp.concatenate([x] * len(head_masks), axis=0), 0.0)

    def diag_blocks(x):
        x = jnp.where(same_head, x, 0.0)
        return x[:hd] + x[hd:]

    step = pl.program_id(0)
    cur = lax.rem(step, 2)
    prev = 1 - cur

    @pl.when(step == 0)
    def _():
        for scr in (q2_scr, y0_scr, m_scr, h0_scr, bv_scr, gate_scr):
            scr[1] = jnp.zeros(scr.shape[1:], scr.dtype)

    state = [jnp.zeros((hd, LANES), F32)]
    carried = [0]

    def carry_one():
        c = carried[0]
        if c < n_chunks:
            hs_scr[c] = state[0]
            state[0] = _dot(m_scr[prev, c], stacked(state[0]), (2, 2)) + h0_scr[prev, c]
            carried[0] = c + 1

    def stage(fn, *lists):
        out = []
        for i, args in enumerate(zip(*lists)):
            out.append(fn(*args))
            if i % CARRY_EVERY == CARRY_EVERY - 1:
                carry_one()
        return out

    chunks = list(range(n_chunks))
    rows = [slice(c * c_len, (c + 1) * c_len) for c in chunks]
    r = [r_ref[rw, :] for rw in rows]
    k = [k_ref[rw, :] for rw in rows]
    v = [v_ref[rw, :] for rw in rows]
    a = [a_ref[rw, :] for rw in rows]
    lw = [lw_ref[rw, :] for rw in rows]
    k2 = [x * (1.0 + (y - 1.0) * ka_ref[...]) for x, y in zip(k, a)]
    kk = [x * kk_ref[...] for x in k]
    norm = stage(lambda x: head_sum(x * x), kk)
    cum = stage(lambda x: _dot(tri, x, (1, 3)), lw)
    bonus = stage(lambda x, y: head_sum(x * y * rk_ref[...]), r, k2)
    kk = [x / jnp.maximum(jnp.sqrt(n), 1e-12) for x, n in zip(kk, norm)]
    last = [x[c_len - 1:c_len, :] for x in cum]
    b = [x * y for x, y in zip(kk, a)]
    ekk = [x * jnp.exp(cm - w) for x, cm, w in zip(kk, cum, lw)]
    er = [x * jnp.exp(cm) for x, cm in zip(r, cum)]
    inv = [jnp.exp(-cm) for cm in cum]
    eb = [x * y for x, y in zip(b, inv)]
    ek = [x * y for x, y in zip(k2, inv)]
    to_end = [jnp.exp(ls) * x for ls, x in zip(last, inv)]
    eb_end = [x * y for x, y in zip(b, to_end)]
    ek_end = [x * y for x, y in zip(k2, to_end)]
    lhs = [jnp.concatenate([x, y], axis=0) for x, y in zip(ekk, er)]
    g = stage(lambda x, y, z: _dot_nt(x, jnp.concatenate([stacked(y), stacked(z)], axis=0)), lhs, eb, ek)
    l_b = [jnp.where(strict, x[:c_len, :LANES], 0.0) for x in g]
    a_b = [jnp.where(incl, x[c_len:, :LANES], 0.0) for x in g]
    l_k = [jnp.where(strict, x[:c_len, LANES:], 0.0) for x in g]
    a_k = [jnp.where(incl, x[c_len:, LANES:], 0.0) for x in g]
    lakv = stage(lambda x, y, z: _dot(jnp.concatenate([x, y], axis=0), stacked(z)), l_k, a_k, v)
    lkv = [x[:c_len] for x in lakv]
    akv = [x[c_len:] for x in lakv]
    levels = int(math.log2(c_len))
    power = [-x for x in l_b]
    inv_t = [eye_c + x for x in power]
    power = stage(lambda x: _dot(x, block_diag(x)), power)
    for level in range(1, levels):
        if level < levels - 1:
            both = stage(lambda x, y: _dot(jnp.concatenate([x, y], axis=0), block_diag(x)), power, inv_t)
            power = [x[:c_len] for x in both]
            inv_t = [x + y[c_len:] for x, y in zip(inv_t, both)]
        else:
            inv_t = stage(lambda x, y: y + _dot(y, block_diag(x)), power, inv_t)
    tz = stage(lambda x, y, z: _dot(x, stacked(jnp.concatenate([y, z], axis=1))), inv_t, ekk, lkv)
    qu = [jnp.concatenate([x[:, :LANES], -x[:, LANES:]], axis=1) for x in tz]
    ab = stage(lambda x, y: _dot(x, stacked(y)), a_b, qu)
    ends = [jnp.concatenate([x, y], axis=0) for x, y in zip(eb_end, ek_end)]
    tails = [jnp.concatenate([x, jnp.concatenate([jnp.zeros_like(y), y], axis=1)], axis=0) for x, y in zip(qu, v)]
    bq = stage(_dot_tn, ends, tails)
    for c in chunks:
        q2_scr[cur, c] = er[c] - ab[c][:, :LANES]
        y0_scr[cur, c] = ab[c][:, LANES:] + akv[c]
        m_scr[cur, c] = eye_c * jnp.exp(last[c]) - diag_blocks(bq[c][:, :LANES])
        h0_scr[cur, c] = diag_blocks(bq[c][:, LANES:])
        bv_scr[cur, rows[c], :] = bonus[c] * v[c]
        gate_scr[cur, rows[c], :] = g_ref[rows[c], :]
    while carried[0] < n_chunks:
        carry_one()
    st_ref[...] = _dot_tn(state[0], (ti == si).astype(F32), (3, 1))

    y = [_dot(q2_scr[prev, c], stacked(hs_scr[c]), (2, 2)) + y0_scr[prev, c] for c in chunks]
    mean = [head_sum(x) * (1.0 / hd) for x in y]
    dev = [x - mu for x, mu in zip(y, mean)]
    var = [head_sum(x * x) * (1.0 / hd) for x in dev]
    for rw, dv, vr in zip(rows, dev, var):
        yn = dv * lax.rsqrt(vr + GN_EPS) * gng_ref[...] + gnb_ref[...]
        o_ref[rw, :] = ((yn + bv_scr[prev, rw, :]) * gate_scr[prev, rw, :]).astype(o_ref.dtype)


def _rwkv_prompt(r, k, v, lw, a, g, k_k, k_a, r_k, gn_g, gn_b, batch, seq):
    d = r.shape[1]
    pairs = d // LANES
    n_seq = batch * pairs
    n_chunks = seq // RWKV_CHUNK
    hd = C_HEAD_DIM
    this = lambda s: jnp.minimum(s, n_seq - 1)
    last = lambda s: jnp.maximum(s - 1, 0)
    blk = pl.BlockSpec((None, seq, LANES), lambda s: (this(s) // pairs, 0, this(s) % pairs))
    vec = pl.BlockSpec((1, LANES), lambda s: (0, this(s) % pairs))
    vec_last = pl.BlockSpec((1, LANES), lambda s: (0, last(s) % pairs))
    r3 = lambda t: t.reshape(batch, seq, d)
    v2 = lambda t: t.reshape(1, d)
    chunk_maps = pltpu.VMEM((2, n_chunks, RWKV_CHUNK, LANES), F32)
    rows_saved = pltpu.VMEM((2, seq, LANES), F32)
    vmem = 2 * 6 * seq * LANES * 4 + 2 * seq * LANES * 2 + (4 * 2 + 2 * 2 + 1) * seq * LANES * 4 + (24 << 20)
    out, state = pl.pallas_call(
        functools.partial(_rwkv_prompt_body, seq=seq),
        grid=(n_seq + 1,),
        in_specs=[blk] * 6 + [vec] * 3 + [vec_last] * 2,
        out_specs=[pl.BlockSpec((None, seq, LANES), lambda s: (last(s) // pairs, 0, last(s) % pairs)),
                   pl.BlockSpec((None, None, LANES, hd), lambda s: (last(s) // pairs, last(s) % pairs, 0, 0))],
        out_shape=[jax.ShapeDtypeStruct((batch, seq, d), BF16),
                   jax.ShapeDtypeStruct((batch, pairs, LANES, hd), F32)],
        scratch_shapes=[chunk_maps] * 4 + [rows_saved] * 2 + [pltpu.VMEM((n_chunks, RWKV_CHUNK, LANES), F32)],
        compiler_params=_params(("arbitrary",), vmem),
        name="rwkv_prompt",
    )(r3(r), r3(k), r3(v), r3(lw), r3(a), r3(g), v2(k_k), v2(k_a), v2(r_k), v2(gn_g), v2(gn_b))
    return out.reshape(batch * seq, d), state.reshape(batch, d // hd, hd, hd)


def _rwkv_sample_body(s_ref, r_ref, k_ref, lw_ref, a_ref, v_ref, kk_ref, ka_ref, rk_ref, gng_ref, gnb_ref,
                      y_ref, so_ref):
    r, k, a = r_ref[...], k_ref[...], a_ref[...]
    v = v_ref[...]
    kk = k * kk_ref[...]
    kk = kk / jnp.maximum(jnp.sqrt(jnp.sum(kk * kk, axis=-1, keepdims=True)), 1e-12)
    k2 = k * (1.0 + (a - 1.0) * ka_ref[...])
    s = s_ref[...]
    sa = -jnp.sum(s * kk, axis=-1, keepdims=True)
    s = s * jnp.exp(lw_ref[...]) + sa * (kk * a) + v * k2
    so_ref[...] = s
    y = jnp.sum(s * r, axis=-1, keepdims=True)
    mean = jnp.mean(y, axis=1, keepdims=True)
    var = jnp.mean(jnp.square(y - mean), axis=1, keepdims=True)
    y = (y - mean) * lax.rsqrt(var + GN_EPS) * gng_ref[...] + gnb_ref[...]
    bonus = jnp.sum(r * k2 * rk_ref[...], axis=-1, keepdims=True)
    y_ref[...] = y + bonus * v


def _rwkv_sample(state, r, k, v, lw, a, k_k, k_a, r_k, gn_g, gn_b):
    nb, heads, hd, _ = state.shape
    rowv = lambda t: t.reshape(nb, heads, 1, hd)
    rowp = lambda t: t.reshape(heads, 1, hd)
    colp = lambda t: t.reshape(heads, hd, 1)
    st = pl.BlockSpec((None, heads, hd, hd), lambda b: (b, 0, 0, 0))
    rv = pl.BlockSpec((None, heads, 1, hd), lambda b: (b, 0, 0, 0))
    cv = pl.BlockSpec((None, heads, hd, 1), lambda b: (b, 0, 0, 0))
    rp = pl.BlockSpec((heads, 1, hd), lambda b: (0, 0, 0))
    cp = pl.BlockSpec((heads, hd, 1), lambda b: (0, 0, 0))
    y, new_state = pl.pallas_call(
        _rwkv_sample_body,
        grid=(nb,),
        in_specs=[st, rv, rv, rv, rv, cv, rp, rp, rp, cp, cp],
        out_specs=[cv, st],
        out_shape=[jax.ShapeDtypeStruct((nb, heads, hd, 1), F32), jax.ShapeDtypeStruct(state.shape, F32)],
        compiler_params=_params(("parallel",), 32 << 20),
        name="rwkv_sample",
    )(state, rowv(r), rowv(k), rowv(lw), rowv(a), v.reshape(nb, heads, hd, 1),
      rowp(k_k), rowp(k_a), rowp(r_k), colp(gn_g), colp(gn_b))
    return y.reshape(nb, heads * hd), new_state


def _gate_cast_body(y_ref, g_ref, o_ref):
    o_ref[...] = (y_ref[...] * g_ref[...]).astype(o_ref.dtype)


def _gate_cast(y, g):
    return pl.pallas_call(_gate_cast_body, out_shape=jax.ShapeDtypeStruct(y.shape, BF16), name="gate_cast")(y, g)


def kernel(x_prompt, x_sample, cache_a_k, cache_a_v, state_c_wkv, state_c_shift, p_prompt, p_sample, norm_g, final_norm_g, rel_bias, ab_w_in, ab_w_out, b_w_s, b_b_s, b_ln_g, b_ln_b, c_mu, c_w_r, c_w_k, c_w_v, c_w_g, c_w_o, c_w0, c_w1, c_w2, c_a0, c_a1, c_a2, c_k_k, c_k_a, c_r_k, c_gn_g, c_gn_b, ple_w_proj, ple_w_gate):
    batch, seq, d = x_prompt.shape
    nb = x_sample.shape[0]
    assert x_sample.shape[1] == 1 and seq % (QBLK * max(dil for _, dil in DILATION_PATTERNS)) == 0
    assert norm_g.shape[0] == 2, "layer pattern implemented for depth 2: one attention+gMLP layer, one RWKV-7 layer"
    a_heads = cache_a_k.shape[3]
    c_heads = state_c_wkv.shape[2]
    m = batch * seq
    hp = x_prompt.reshape(m, d)
    hs = x_sample.reshape(nb, d)
    pp = p_prompt.reshape(p_prompt.shape[0], m, -1)
    ps = p_sample.reshape(p_sample.shape[0], nb, -1)
    w_gate = ple_w_gate.astype(BF16)
    w_proj = ple_w_proj.astype(BF16)

    w_in = ab_w_in[0].astype(BF16)
    w_out = ab_w_out[0].astype(BF16)
    bias_prompt = _prompt_bias(rel_bias)
    bias_cache, bias_own = _sample_bias(rel_bias)

    q, k, v, ga, ub, vn, gb = _ab_in_proj(hp, norm_g[0], w_in, b_ln_g[0], b_ln_b[0], tm=256)
    oa = _attn_prompt(q, k, v, ga, bias_prompt, batch, seq)
    ob = _gmlp_prompt(vn, ub, gb, b_w_s[0], b_b_s[0], tm=512)
    hp = _out_ple([oa, ob], hp, pp[0], w_out, w_gate[0], w_proj[0], None, tm=256)
    a_k_p = k.reshape(1, batch, seq, a_heads, A_HEAD_DIM)
    a_v_p = v.reshape(1, batch, seq, a_heads, A_HEAD_DIM)

    qs, ks, vs, gas, ubs, vns, gbs = _ab_in_proj(hs, norm_g[0], w_in, b_ln_g[0], b_ln_b[0], tm=nb)
    oas = _attn_sample(qs, ks, vs, gas, cache_a_k[0], cache_a_v[0], bias_cache, bias_own)
    obs = _gmlp_sample(vns, ubs, gbs, b_w_s[0], b_b_s[0])
    hs = _out_ple([oas, obs], hs, ps[0], w_out, w_gate[0], w_proj[0], None, tm=nb)
    a_k_s = ks.reshape(1, nb, 1, a_heads, A_HEAD_DIM)
    a_v_s = vs.reshape(1, nb, 1, a_heads, A_HEAD_DIM)
    b_v_s = vns.reshape(1, nb, 1, -1)

    mu = c_mu[0]
    w_r, w_k, w_v, w_g, w_o = (t[0].astype(BF16) for t in (c_w_r, c_w_k, c_w_v, c_w_g, c_w_o))
    rk_flat = c_r_k[0].reshape(-1)

    def projections(h, prev, tm, seq_rows):
        r = _mix_mm(h, prev, norm_g[1], mu[0], w_r, None, tm, seq_rows)
        kx = _mix_mm(h, prev, norm_g[1], mu[2], w_k, None, tm, seq_rows)
        vx = _mix_mm(h, prev, norm_g[1], mu[3], w_v, None, tm, seq_rows)
        g = _mix_mm(h, prev, norm_g[1], mu[5], w_g, _silu, tm, seq_rows)
        lw, a = _lora(h, prev, norm_g[1], mu[1], mu[4], c_w1[0], c_w2[0], c_w0[0], c_a1[0], c_a2[0], c_a0[0],
                      min(tm, 256), seq_rows)
        return r, kx, vx, g, lw, a

    r, kx, vx, g, lw, a = projections(hp, None, 512, seq)
    yg, s_p = _rwkv_prompt(r, kx, vx, lw, a, g, c_k_k[0], c_k_a[0], rk_flat, c_gn_g[0], c_gn_b[0], batch, seq)
    y_prompt = _out_ple([yg], hp, pp[1], w_o, w_gate[1], w_proj[1], final_norm_g, tm=256)
    last_rows = hp.reshape(batch, seq, d)[:, seq - SUBLANES:, :].reshape(batch * SUBLANES, d)
    sh_p = _norm_rows(last_rows, norm_g[1]).reshape(batch, SUBLANES, d)[:, SUBLANES - 1]

    rs, kxs, vxs, gs, lws, a_s = projections(hs, state_c_shift[0], nb, 1)
    ys, s_s = _rwkv_sample(state_c_wkv[0], rs, kxs, vxs, lws, a_s, c_k_k[0], c_k_a[0], rk_flat, c_gn_g[0], c_gn_b[0])
    y_sample = _out_ple([_gate_cast(ys, gs)], hs, ps[1], w_o, w_gate[1], w_proj[1], final_norm_g, tm=nb)
    sh_s = _norm_rows(hs, norm_g[1])

    return (y_prompt.reshape(batch, seq, d), y_sample.reshape(nb, 1, d), a_k_p, a_v_p, a_k_s, a_v_s, b_v_s,
            s_p[None], sh_p[None], s_s[None], sh_s[None])


def _norm_rows_body(x_ref, g_ref, o_ref):
    o_ref[...] = _rms_norm(x_ref[...], g_ref[...])


def _norm_rows(x, g):
    return pl.pallas_call(_norm_rows_body, out_shape=jax.ShapeDtypeStruct(x.shape, F32), name="norm_rows")(
        x, g.reshape(1, -1))
```

```python
import functools
import math

import jax
import jax.numpy as jnp
from jax import lax
from jax.experimental import pallas as pl
from jax.experimental.pallas import tpu as pltpu

F32 = jnp.float32
BF16 = jnp.bfloat16

LANES = 128
SUBLANES = 8
VMEM_BUDGET_BYTES = 56 * 1024 * 1024

A_HEAD_DIM = 128
DILATION_PATTERNS = ((128, 1), (512, 4), (2048, 16))
QBLK = 128
ATTN_GROUP = 8
REL_BUCKETS = 32
REL_MAX_DIST = 2048
CHUNK = 128
C_HEAD_DIM = 64
CARRY_EVERY = 12
RWKV_CHUNK = 64
RMS_EPS = 1e-6
LN_EPS = 1e-5
GN_EPS = 64e-5
NEG_INF = -1e30


def _params(semantics, vmem_bytes):
    return pltpu.CompilerParams(dimension_semantics=semantics, vmem_limit_bytes=int(vmem_bytes))


def _resident(shape, index_map):
    return pl.BlockSpec(shape, index_map, pipeline_mode=pl.Buffered(1))


def _bf16_terms(x, n):
    if x.dtype == BF16 or n == 1:
        return [x.astype(BF16)]
    terms, rest = [], x
    for _ in range(n):
        terms.append(rest.astype(BF16))
        rest = rest - terms[-1].astype(F32)
    return terms


def _dot_dims(a, b, dims, terms):
    a_terms = _bf16_terms(a, terms[0])
    b_terms = _bf16_terms(b, terms[1])
    out = None
    for i, at in enumerate(a_terms):
        for j, bt in enumerate(b_terms):
            if i + j < max(len(a_terms), len(b_terms)):
                part = lax.dot_general(at, bt, (dims, ((), ())), preferred_element_type=F32)
                out = part if out is None else out + part
    return out


def _dot(a, b, terms=(1, 1)):
    return _dot_dims(a, b, ((1,), (0,)), terms)


def _dot_nt(a, b, terms=(1, 1)):
    return _dot_dims(a, b, ((1,), (1,)), terms)


def _dot_tn(a, b, terms=(1, 1)):
    return _dot_dims(a, b, ((0,), (0,)), terms)


def _rms_norm(x, g):
    return x * lax.rsqrt(jnp.mean(x * x, axis=-1, keepdims=True) + RMS_EPS) * g


def _layer_norm(x, g, b):
    mu = jnp.mean(x, axis=-1, keepdims=True)
    var = jnp.mean(jnp.square(x - mu), axis=-1, keepdims=True)
    return (x - mu) * lax.rsqrt(var + LN_EPS) * g + b


def _silu(x):
    return x * jax.nn.sigmoid(x)


def _ab_in_body(x_ref, g_ref, w_ref, lng_ref, lnb_ref, *rest, mix_chunks):
    xn = _rms_norm(x_ref[...], g_ref[...]).astype(BF16)
    if mix_chunks:
        ws_ref, bs_ref, q_ref, k_ref, v_ref, ga_ref, ob_ref = rest
    else:
        q_ref, k_ref, v_ref, ga_ref, ub_ref, vn_ref, gb_ref = rest
    width = q_ref.shape[1]
    column_group = lambda idx: _dot(xn, w_ref[:, idx * width:(idx + 1) * width])
    q_ref[...] = column_group(0)
    k_ref[...] = column_group(1)
    v_ref[...] = column_group(2)
    ga_ref[...] = _silu(column_group(3))
    ub = jax.nn.gelu(column_group(4))
    vn = _layer_norm(jax.nn.gelu(column_group(5)), lng_ref[...], lnb_ref[...])
    gb = _silu(column_group(6))
    if not mix_chunks:
        ub_ref[...], vn_ref[...], gb_ref[...] = ub, vn, gb
        return
    ii = lax.broadcasted_iota(jnp.int32, (CHUNK, CHUNK), 0)
    jj = lax.broadcasted_iota(jnp.int32, (CHUNK, CHUNK), 1)
    vn = vn.astype(BF16)
    for grp in range(ws_ref.shape[0]):
        w = jnp.where(jj <= ii, ws_ref[grp], 0.0).astype(BF16)
        cols = slice(grp * LANES, (grp + 1) * LANES)
        for c in range(xn.shape[0] // CHUNK):
            rows = slice(c * CHUNK, (c + 1) * CHUNK)
            s = _dot(w, vn[rows, cols]) + bs_ref[:, cols]
            ob_ref[rows, cols] = (ub[rows, cols] * s * gb[rows, cols]).astype(ob_ref.dtype)


def _ab_in_proj(x, norm_g, w_in_bf16, ln_g, ln_b, tm, w_s=None, b_s=None):
    m, d = x.shape
    width = w_in_bf16.shape[1] // 7
    out = jax.ShapeDtypeStruct((m, width), F32)
    row = lambda i: (i, 0)
    fixed = lambda i: (0, 0)
    mix_chunks = w_s is not None
    in_specs = [pl.BlockSpec((tm, d), row), pl.BlockSpec((1, d), fixed), _resident(w_in_bf16.shape, fixed),
                pl.BlockSpec((1, width), fixed), pl.BlockSpec((1, width), fixed)]
    args = [x, norm_g.reshape(1, d), w_in_bf16, ln_g.reshape(1, width), ln_b.reshape(1, width)]
    if mix_chunks:
        assert tm % CHUNK == 0 and width == w_s.shape[0] * LANES
        groups = w_s.shape[0]
        in_specs += [pl.BlockSpec((groups, CHUNK, CHUNK), lambda i: (0, 0, 0)), pl.BlockSpec((CHUNK, width), fixed)]
        args += [w_s, jnp.repeat(b_s.T, width // groups, axis=1)]
        out_shape = [out] * 4 + [jax.ShapeDtypeStruct((m, width), BF16)]
    else:
        out_shape = [out] * 7
    vmem = w_in_bf16.size * 2 + 2 * tm * d * 4 + len(out_shape) * 2 * tm * width * 4 + tm * d * 2 + 10 * tm * width * 4
    return pl.pallas_call(
        functools.partial(_ab_in_body, mix_chunks=mix_chunks),
        grid=(m // tm,),
        in_specs=in_specs,
        out_specs=[pl.BlockSpec((tm, width), row)] * len(out_shape),
        out_shape=out_shape,
        compiler_params=_params(("parallel",), vmem),
        name="ab_in_proj",
    )(*args)


def _t5_bucket(dist):
    n_exact = REL_BUCKETS // 2
    d = jnp.maximum(dist, 1).astype(F32)
    log_b = n_exact + (jnp.log(d / n_exact) / math.log(REL_MAX_DIST / n_exact) * (REL_BUCKETS - n_exact)).astype(jnp.int32)
    return jnp.where(dist < n_exact, dist, jnp.minimum(log_b, REL_BUCKETS - 1))


def _bias_at(rel_bias, dist):
    one_hot = jax.nn.one_hot(_t5_bucket(dist), REL_BUCKETS, dtype=F32)
    return jnp.einsum("...k,kh->...h", one_hot, rel_bias.astype(F32), precision=lax.Precision.HIGHEST)


def _prompt_bias(rel_bias):
    i = jnp.arange(QBLK)[:, None]
    j = jnp.arange(2 * QBLK)[None, :]
    steps = QBLK + i - j
    tables = []
    for window, dil in DILATION_PATTERNS:
        band = (steps >= 0) & (steps <= window // dil)
        bias = jnp.moveaxis(_bias_at(rel_bias, jnp.clip(steps, 0) * dil), -1, 0)
        tables.append(jnp.where(band[None], bias, NEG_INF))
    return jnp.stack(tables)


def _sample_bias(rel_bias):
    back = QBLK - jnp.arange(QBLK)
    cache = jnp.stack([_bias_at(rel_bias, back * dil) for _, dil in DILATION_PATTERNS])
    own = _bias_at(rel_bias, jnp.zeros((), jnp.int32))
    return cache, own


def _attn_prompt_body(q_ref, k_ref, v_ref, gate_ref, bias_ref, o_ref, o_scr, lse_scr, *, seq):
    scale = A_HEAD_DIM ** -0.5

    def blocks(p, dil, starts, with_prev):
        ds = lambda st: pl.ds(st, QBLK, stride=dil) if dil > 1 else pl.ds(st, QBLK)
        rows = [ds(st) for st in starts]
        q = [q_ref[rw, :].astype(BF16) for rw in rows]
        k = [k_ref[rw, :] for rw in rows]
        v = [v_ref[rw, :] for rw in rows]
        if with_prev:
            prev = [ds(st - QBLK * dil) for st in starts]
            k = [jnp.concatenate([k_ref[pv, :], x], axis=0) for pv, x in zip(prev, k)]
            v = [jnp.concatenate([v_ref[pv, :], x], axis=0) for pv, x in zip(prev, v)]
            bias = bias_ref[p]
        else:
            bias = bias_ref[p, :, QBLK:]
        s = [_dot_nt(x, y) * scale + bias for x, y in zip(q, k)]
        m = [jnp.max(x, axis=-1, keepdims=True) for x in s]
        e = [jnp.exp(x - y) for x, y in zip(s, m)]
        l = [jnp.sum(x, axis=-1, keepdims=True) for x in e]
        o = [_dot(x, y) / z for x, y, z in zip(e, v, l)]
        for rw, x, y, z in zip(rows, o, m, l):
            o_scr[p, rw, :] = x
            lse_scr[p, rw, :] = jnp.broadcast_to(y + jnp.log(z), (QBLK, A_HEAD_DIM))

    def widest_group(count):
        return max(g for g in range(1, ATTN_GROUP + 1) if count % g == 0)

    for p, (_, dil) in enumerate(DILATION_PATTERNS):
        nb = seq // (QBLK * dil)
        shift = dil.bit_length() - 1
        g_first = widest_group(dil)

        def first(i, carry, p=p, dil=dil, g=g_first):
            blocks(p, dil, [i * g + u for u in range(g)], False)
            return carry

        lax.fori_loop(0, dil // g_first, first, 0)
        if nb > 1:
            g_rest = widest_group(dil * (nb - 1))

            def rest(i, carry, p=p, dil=dil, shift=shift, g=g_rest):
                idx = [i * g + u for u in range(g)]
                blocks(p, dil, [((x >> shift) + 1) * (QBLK * dil) + (x & (dil - 1)) for x in idx], True)
                return carry

            lax.fori_loop(0, dil * (nb - 1) // g_rest, rest, 0)

    lse = [lse_scr[p] for p in range(len(DILATION_PATTERNS))]
    m = functools.reduce(jnp.maximum, lse)
    e = [jnp.exp(t - m) for t in lse]
    den = functools.reduce(jnp.add, e)
    mix = functools.reduce(jnp.add, [(e[p] / den) * o_scr[p] for p in range(len(e))])
    o_ref[...] = (mix * gate_ref[...]).astype(o_ref.dtype)


def _attn_prompt(q, k, v, gate, bias, batch, seq):
    width = q.shape[1]
    heads = width // A_HEAD_DIM
    n_pat = len(DILATION_PATTERNS)
    blk = pl.BlockSpec((None, seq, A_HEAD_DIM), lambda b, h: (b, 0, h))
    r3 = lambda t: t.reshape(batch, seq, width)
    vmem = 4 * 2 * seq * A_HEAD_DIM * 4 + 2 * seq * A_HEAD_DIM * 2 + 2 * n_pat * seq * A_HEAD_DIM * 4 \
        + 2 * n_pat * QBLK * 2 * QBLK * 4 + (8 << 20)
    out = pl.pallas_call(
        functools.partial(_attn_prompt_body, seq=seq),
        grid=(batch, heads),
        in_specs=[blk, blk, blk, blk,
                  pl.BlockSpec((n_pat, None, QBLK, 2 * QBLK), lambda b, h: (0, h, 0, 0))],
        out_specs=blk,
        out_shape=jax.ShapeDtypeStruct((batch, seq, width), BF16),
        scratch_shapes=[pltpu.VMEM((n_pat, seq, A_HEAD_DIM), F32), pltpu.VMEM((n_pat, seq, A_HEAD_DIM), F32)],
        compiler_params=_params(("parallel", "parallel"), vmem),
        name="attn_prompt",
    )(r3(q), r3(k), r3(v), r3(gate), bias)
    return out.reshape(batch * seq, width)


def _attn_sample_body(q_ref, kn_ref, vn_ref, gate_ref, bc_ref, bo_ref, *rest):
    n_pat = len(DILATION_PATTERNS)
    kc_refs, vc_refs, o_ref = rest[:n_pat], rest[n_pat:2 * n_pat], rest[2 * n_pat]
    scale = A_HEAD_DIM ** -0.5
    rnd = lambda t: t.astype(BF16).astype(F32)
    q = rnd(q_ref[...])
    kn = rnd(kn_ref[...])
    vn = rnd(vn_ref[...])
    s_new = jnp.sum(q * kn, axis=-1, keepdims=True) * scale + bo_ref[...]
    outs, lses = [], []
    for p in range(n_pat):
        kc = rnd(kc_refs[p][...])
        vc = rnd(vc_refs[p][...])
        s = jnp.sum(q[None] * kc, axis=-1, keepdims=True) * scale + bc_ref[p]
        m = jnp.maximum(jnp.max(s, axis=0), s_new)
        e = jnp.exp(s - m[None])
        e_new = jnp.exp(s_new - m)
        l = jnp.sum(e, axis=0) + e_new
        outs.append((jnp.sum(rnd(e) * vc, axis=0) + rnd(e_new) * vn) / l)
        lses.append(m + jnp.log(l))
    m = functools.reduce(jnp.maximum, lses)
    e = [jnp.exp(t - m) for t in lses]
    den = functools.reduce(jnp.add, e)
    mix = functools.reduce(jnp.add, [(e[p] / den) * outs[p] for p in range(n_pat)])
    o_ref[...] = (mix * gate_ref[...]).astype(o_ref.dtype)


def _attn_sample(q, k_new, v_new, gate, cache_k, cache_v, bias_cache, bias_own):
    nb, past, heads, hd = cache_k.shape
    n_pat = len(DILATION_PATTERNS)
    row = pl.BlockSpec((None, heads, hd), lambda b: (b, 0, 0))
    r3 = lambda t: t.reshape(nb, heads, hd)
    cache_specs, cache_args = [], []
    for cache in (cache_k, cache_v):
        for window, dil in DILATION_PATTERNS:
            last = past // (QBLK * dil) - 1
            cache_specs.append(pl.BlockSpec((None, QBLK, None, heads, hd), lambda b, last=last: (b, last, 0, 0, 0)))
            cache_args.append(cache.reshape(nb, past // dil, dil, heads, hd))
    vmem = 2 * 2 * n_pat * QBLK * heads * hd * 4 + 8 * QBLK * heads * hd * 4 + (8 << 20)
    out = pl.pallas_call(
        _attn_sample_body,
        grid=(nb,),
        in_specs=[row, row, row, row,
                  pl.BlockSpec((n_pat, QBLK, heads, 1), lambda b: (0, 0, 0, 0)),
                  pl.BlockSpec((heads, 1), lambda b: (0, 0))] + cache_specs,
        out_specs=row,
        out_shape=jax.ShapeDtypeStruct((nb, heads, hd), BF16),
        compiler_params=_params(("parallel",), vmem),
        name="attn_sample",
    )(r3(q), r3(k_new), r3(v_new), r3(gate), bias_cache[..., None], bias_own.reshape(heads, 1), *cache_args)
    return out.reshape(nb, heads * hd)


def _gmlp_sample_body(vn_ref, ub_ref, gb_ref, w0_ref, b0_ref, o_ref):
    s = w0_ref[...].astype(BF16).astype(F32) * vn_ref[...].astype(BF16).astype(F32) + b0_ref[...]
    o_ref[...] = (ub_ref[...] * s * gb_ref[...]).astype(o_ref.dtype)


def _gmlp_sample(vn, ub, gb, w_s, b_s):
    m, width = vn.shape
    groups = w_s.shape[0]
    w0 = jnp.repeat(w_s[:, 0, 0], width // groups).reshape(1, width)
    b0 = jnp.repeat(b_s[:, 0], width // groups).reshape(1, width)
    return pl.pallas_call(
        _gmlp_sample_body,
        out_shape=jax.ShapeDtypeStruct((m, width), BF16),
        name="gmlp_sample",
    )(vn, ub, gb, w0, b0)


def _out_ple_body(*refs, n_a, final_norm):
    a_refs = refs[:n_a]
    x_ref, p_ref, wo_ref, wg_ref, wp_ref = refs[n_a:n_a + 5]
    rest = refs[n_a + 5:]
    h = x_ref[...]
    off = 0
    for a_ref in a_refs:
        ka = a_ref.shape[1]
        h = h + _dot(a_ref[...], wo_ref[off:off + ka, :])
        off += ka
    gate = jax.nn.sigmoid(_dot(h.astype(BF16), wg_ref[...]))
    h = h + gate * _dot(p_ref[...].astype(BF16), wp_ref[...])
    if final_norm:
        fg_ref, o_ref = rest
        o_ref[...] = _rms_norm(h, fg_ref[...])
    else:
        (o_ref,) = rest
        o_ref[...] = h


def _out_ple(a_list, x, p_layers, layer, w_out_bf16, w_gate_layers, w_proj_layers, final_g, tm):
    m, d = x.shape
    pd = p_layers.shape[2]
    row = lambda i: (i, 0)
    fixed = lambda i: (0, 0)
    of_layer = lambda i: (layer, 0, 0)
    in_specs = [pl.BlockSpec((tm, a.shape[1]), row) for a in a_list]
    in_specs += [pl.BlockSpec((tm, d), row), pl.BlockSpec((None, tm, pd), lambda i: (layer, i, 0)),
                 _resident(w_out_bf16.shape, fixed), _resident((None, d, d), of_layer), _resident((None, pd, d), of_layer)]
    args = list(a_list) + [x, p_layers, w_out_bf16, w_gate_layers, w_proj_layers]
    if final_g is not None:
        in_specs.append(pl.BlockSpec((1, d), fixed))
        args.append(final_g.reshape(1, d))
    vmem = (w_out_bf16.size + d * d + pd * d) * 2 + 2 * tm * (2 * d * 4 + pd * 4 + w_out_bf16.shape[0] * 2) \
        + 4 * tm * d * 4 + (6 << 20)
    return pl.pallas_call(
        functools.partial(_out_ple_body, n_a=len(a_list), final_norm=final_g is not None),
        grid=(m // tm,),
        in_specs=in_specs,
        out_specs=pl.BlockSpec((tm, d), row),
        out_shape=jax.ShapeDtypeStruct((m, d), F32),
        compiler_params=_params(("parallel",), vmem),
        name="out_ple",
    )(*args)


def _shifted_inputs(h_ref, prev_ref, g_ref, *, seq_rows, prev_given):
    xn = _rms_norm(h_ref[...], g_ref[...])
    if prev_given:
        return xn, prev_ref[...]
    tm = xn.shape[0]
    before = _rms_norm(prev_ref[...], g_ref[...])[SUBLANES - 1:SUBLANES, :]
    starts_sequence = (pl.program_id(0) * tm) % seq_rows == 0
    before = jnp.where(starts_sequence, 0.0, before)
    first = lax.broadcasted_iota(jnp.int32, xn.shape, 0) == 0
    return xn, jnp.where(first, before, pltpu.roll(xn, 1, 0))


def _rwkv_proj_body(h_ref, prev_ref, g_ref, mu_ref, w_ref, *rest, seq_rows, prev_given, mixes, acts, lora_mixes):
    xn, x_prev = _shifted_inputs(h_ref, prev_ref, g_ref, seq_rows=seq_rows, prev_given=prev_given)
    xx = x_prev - xn
    mixed = lambda idx: (xn + xx * mu_ref[idx:idx + 1, :]).astype(BF16)
    if lora_mixes is not None:
        w1_ref, w2_ref, w0_ref, a1_ref, a2_ref, a0_ref = rest[:6]
        rest = rest[6:]
    for i, (mix, act) in enumerate(zip(mixes, acts)):
        z = _dot(mixed(mix), w_ref[i])
        rest[i][...] = act(z) if act is not None else z
    if lora_mixes is not None:
        lw_ref, a_ref = rest[len(mixes):]
        zw = w0_ref[...] + _dot(jnp.tanh(_dot(mixed(lora_mixes[0]), w1_ref[...])), w2_ref[...])
        lw_ref[...] = -math.exp(-0.5) * jax.nn.sigmoid(zw)
        za = a0_ref[...] + _dot(_dot(mixed(lora_mixes[1]), a1_ref[...]), a2_ref[...])
        a_ref[...] = jax.nn.sigmoid(za)


def _pad_lora(w_down, w_up):
    rank = w_down.shape[1]
    pad = (-rank) % LANES
    return (jnp.pad(w_down, ((0, 0), (0, pad))).astype(BF16), jnp.pad(w_up, ((0, pad), (0, 0))).astype(BF16))


def _rwkv_proj(h, prev, norm_g, mu, weights, mixes, acts, lora, lora_mixes, tm, seq_rows):
    m, d = h.shape
    prev_given = prev is not None
    row = lambda i: (i, 0)
    fixed = lambda i: (0, 0)
    if prev_given:
        specs, args = [pl.BlockSpec((tm, d), row), pl.BlockSpec((tm, d), row)], [h, prev]
    else:
        per = tm // SUBLANES
        specs = [pl.BlockSpec((tm, d), row), pl.BlockSpec((SUBLANES, d), lambda i: (jnp.maximum(i * per - 1, 0), 0))]
        args = [h, h]
    w_stack = jnp.stack(weights).astype(BF16)
    specs += [pl.BlockSpec((1, d), fixed), pl.BlockSpec(mu.shape, fixed), _resident(w_stack.shape, lambda i: (0, 0, 0))]
    args += [norm_g.reshape(1, d), mu, w_stack]
    n_out = len(weights)
    lora_bytes = 0
    if lora is not None:
        w1, w2, w0, a1, a2, a0 = lora
        w1p, w2p = _pad_lora(w1, w2)
        a1p, a2p = _pad_lora(a1, a2)
        rank = w1p.shape[1]
        vec, down, up = pl.BlockSpec((1, d), fixed), pl.BlockSpec((d, rank), fixed), pl.BlockSpec((rank, d), fixed)
        specs += [down, up, vec, down, up, vec]
        args += [w1p, w2p, w0.reshape(1, d), a1p, a2p, a0.reshape(1, d)]
        n_out += 2
        lora_bytes = 2 * 4 * d * rank * 2
    out = jax.ShapeDtypeStruct((m, d), F32)
    vmem = w_stack.size * 2 + lora_bytes + (2 + 2 * n_out + 8) * tm * d * 4 + (2 << 20)
    return pl.pallas_call(
        functools.partial(_rwkv_proj_body, seq_rows=seq_rows, prev_given=prev_given, mixes=mixes, acts=acts,
                          lora_mixes=lora_mixes if lora is not None else None),
        grid=(m // tm,),
        in_specs=specs,
        out_specs=[pl.BlockSpec((tm, d), row)] * n_out,
        out_shape=[out] * n_out,
        compiler_params=_params(("parallel",), vmem),
        name="rwkv_proj",
    )(*args)


def _rwkv_prompt_body(r_ref, k_ref, v_ref, lw_ref, a_ref, g_ref, kk_ref, ka_ref, rk_ref, gng_ref, gnb_ref,
                      o_ref, st_ref, q2_scr, y0_scr, m_scr, h0_scr, bv_scr, gate_scr, hs_scr, *, seq):
    c_len = RWKV_CHUNK
    n_chunks = seq // c_len
    hd = C_HEAD_DIM
    assert c_len == hd and LANES % hd == 0, "time x time and key x value blocks share one lane tiling"
    lane = lax.broadcasted_iota(jnp.int32, (1, LANES), 1)
    head_masks = (lane < hd, lane >= hd)
    ti = lax.broadcasted_iota(jnp.int32, (c_len, c_len), 0)
    si = lax.broadcasted_iota(jnp.int32, (c_len, c_len), 1)
    tri = (ti >= si).astype(F32)
    tp = lax.broadcasted_iota(jnp.int32, (c_len, LANES), 0)
    sp = lax.broadcasted_iota(jnp.int32, (c_len, LANES), 1) % hd
    incl = tp >= sp
    strict = tp > sp
    eye_c = (tp == sp).astype(F32)
    bi = lax.broadcasted_iota(jnp.int32, (LANES, LANES), 0)
    bj = lax.broadcasted_iota(jnp.int32, (LANES, LANES), 1)
    same_head = (bi // hd) == (bj // hd)
    head_ones = same_head.astype(BF16)

    def head_sum(x):
        return _dot(x, head_ones, (2, 1))

    def stacked(z):
        masks = [jnp.concatenate([mk] * (z.shape[1] // LANES), axis=1) for mk in head_masks]
        return jnp.concatenate([jnp.where(mk, z, 0.0) for mk in masks], axis=0)

    def block_diag(x):
        return jnp.where(same_head, jnp.concatenate([x] * len(head_masks), axis=0), 0.0)

    def diag_blocks(x):
        x = jnp.where(same_head, x, 0.0)
        return x[:hd] + x[hd:]

    step = pl.program_id(0)
    cur = lax.rem(step, 2)
    prev = 1 - cur

    @pl.when(step == 0)
    def _():
        for scr in (q2_scr, y0_scr, m_scr, h0_scr, bv_scr, gate_scr):
            scr[1] = jnp.zeros(scr.shape[1:], scr.dtype)

    state = [jnp.zeros((hd, LANES), F32)]
    carried = [0]

    def carry_one():
        c = carried[0]
        if c < n_chunks:
            hs_scr[c] = state[0]
            state[0] = _dot(m_scr[prev, c], stacked(state[0]), (2, 2)) + h0_scr[prev, c]
            carried[0] = c + 1

    def stage(fn, *lists):
        out = []
        for i, args in enumerate(zip(*lists)):
            out.append(fn(*args))
            if i % CARRY_EVERY == CARRY_EVERY - 1:
                carry_one()
        return out

    chunks = list(range(n_chunks))
    rows = [slice(c * c_len, (c + 1) * c_len) for c in chunks]
    r = [r_ref[rw, :] for rw in rows]
    k = [k_ref[rw, :] for rw in rows]
    v = [v_ref[rw, :] for rw in rows]
    a = [a_ref[rw, :] for rw in rows]
    lw = [lw_ref[rw, :] for rw in rows]
    k2 = [x * (1.0 + (y - 1.0) * ka_ref[...]) for x, y in zip(k, a)]
    kk = [x * kk_ref[...] for x in k]
    norm = stage(lambda x: head_sum(x * x), kk)
    cum = stage(lambda x: _dot(tri, x, (1, 3)), lw)
    bonus = stage(lambda x, y: head_sum(x * y * rk_ref[...]), r, k2)
    kk = [x / jnp.maximum(jnp.sqrt(n), 1e-12) for x, n in zip(kk, norm)]
    last = [x[c_len - 1:c_len, :] for x in cum]
    b = [x * y for x, y in zip(kk, a)]
    ekk = [x * jnp.exp(cm - w) for x, cm, w in zip(kk, cum, lw)]
    er = [x * jnp.exp(cm) for x, cm in zip(r, cum)]
    inv = [jnp.exp(-cm) for cm in cum]
    eb = [x * y for x, y in zip(b, inv)]
    ek = [x * y for x, y in zip(k2, inv)]
    to_end = [jnp.exp(ls) * x for ls, x in zip(last, inv)]
    eb_end = [x * y for x, y in zip(b, to_end)]
    ek_end = [x * y for x, y in zip(k2, to_end)]
    lhs = [jnp.concatenate([x, y], axis=0) for x, y in zip(ekk, er)]
    g = stage(lambda x, y, z: _dot_nt(x, jnp.concatenate([stacked(y), stacked(z)], axis=0)), lhs, eb, ek)
    l_b = [jnp.where(strict, x[:c_len, :LANES], 0.0) for x in g]
    a_b = [jnp.where(incl, x[c_len:, :LANES], 0.0) for x in g]
    l_k = [jnp.where(strict, x[:c_len, LANES:], 0.0) for x in g]
    a_k = [jnp.where(incl, x[c_len:, LANES:], 0.0) for x in g]
    lakv = stage(lambda x, y, z: _dot(jnp.concatenate([x, y], axis=0), stacked(z)), l_k, a_k, v)
    lkv = [x[:c_len] for x in lakv]
    akv = [x[c_len:] for x in lakv]
    levels = int(math.log2(c_len))
    power = [-x for x in l_b]
    inv_t = [eye_c + x for x in power]
    power = stage(lambda x: _dot(x, block_diag(x)), power)
    for level in range(1, levels):
        if level < levels - 1:
            both = stage(lambda x, y: _dot(jnp.concatenate([x, y], axis=0), block_diag(x)), power, inv_t)
            power = [x[:c_len] for x in both]
            inv_t = [x + y[c_len:] for x, y in zip(inv_t, both)]
        else:
            inv_t = stage(lambda x, y: y + _dot(y, block_diag(x)), power, inv_t)
    tz = stage(lambda x, y, z: _dot(x, stacked(jnp.concatenate([y, z], axis=1))), inv_t, ekk, lkv)
    qu = [jnp.concatenate([x[:, :LANES], -x[:, LANES:]], axis=1) for x in tz]
    ab = stage(lambda x, y: _dot(x, stacked(y)), a_b, qu)
    ends = [jnp.concatenate([x, y], axis=0) for x, y in zip(eb_end, ek_end)]
    tails = [jnp.concatenate([x, jnp.concatenate([jnp.zeros_like(y), y], axis=1)], axis=0) for x, y in zip(qu, v)]
    bq = stage(_dot_tn, ends, tails)
    for c in chunks:
        q2_scr[cur, c] = er[c] - ab[c][:, :LANES]
        y0_scr[cur, c] = ab[c][:, LANES:] + akv[c]
        m_scr[cur, c] = eye_c * jnp.exp(last[c]) - diag_blocks(bq[c][:, :LANES])
        h0_scr[cur, c] = diag_blocks(bq[c][:, LANES:])
        bv_scr[cur, rows[c], :] = bonus[c] * v[c]
        gate_scr[cur, rows[c], :] = g_ref[rows[c], :]
    while carried[0] < n_chunks:
        carry_one()
    st_ref[...] = _dot_tn(state[0], (ti == si).astype(F32), (3, 1))

    y = [_dot(q2_scr[prev, c], stacked(hs_scr[c]), (2, 2)) + y0_scr[prev, c] for c in chunks]
    mean = [head_sum(x) * (1.0 / hd) for x in y]
    dev = [x - mu for x, mu in zip(y, mean)]
    var = [head_sum(x * x) * (1.0 / hd) for x in dev]
    for rw, dv, vr in zip(rows, dev, var):
        yn = dv * lax.rsqrt(vr + GN_EPS) * gng_ref[...] + gnb_ref[...]
        o_ref[rw, :] = ((yn + bv_scr[prev, rw, :]) * gate_scr[prev, rw, :]).astype(o_ref.dtype)


def _rwkv_prompt(r, k, v, lw, a, g, k_k, k_a, r_k, gn_g, gn_b, batch, seq):
    d = r.shape[1]
    pairs = d // LANES
    n_seq = batch * pairs
    n_chunks = seq // RWKV_CHUNK
    hd = C_HEAD_DIM
    this = lambda s: jnp.minimum(s, n_seq - 1)
    last = lambda s: jnp.maximum(s - 1, 0)
    blk = pl.BlockSpec((None, seq, LANES), lambda s: (this(s) // pairs, 0, this(s) % pairs))
    vec = pl.BlockSpec((1, LANES), lambda s: (0, this(s) % pairs))
    vec_last = pl.BlockSpec((1, LANES), lambda s: (0, last(s) % pairs))
    r3 = lambda t: t.reshape(batch, seq, d)
    v2 = lambda t: t.reshape(1, d)
    chunk_maps = pltpu.VMEM((2, n_chunks, RWKV_CHUNK, LANES), F32)
    rows_saved = pltpu.VMEM((2, seq, LANES), F32)
    vmem = 2 * 6 * seq * LANES * 4 + 2 * seq * LANES * 2 + (4 * 2 + 2 * 2 + 1) * seq * LANES * 4 + (24 << 20)
    out, state = pl.pallas_call(
        functools.partial(_rwkv_prompt_body, seq=seq),
        grid=(n_seq + 1,),
        in_specs=[blk] * 6 + [vec] * 3 + [vec_last] * 2,
        out_specs=[pl.BlockSpec((None, seq, LANES), lambda s: (last(s) // pairs, 0, last(s) % pairs)),
                   pl.BlockSpec((None, None, LANES, hd), lambda s: (last(s) // pairs, last(s) % pairs, 0, 0))],
        out_shape=[jax.ShapeDtypeStruct((batch, seq, d), BF16),
                   jax.ShapeDtypeStruct((batch, pairs, LANES, hd), F32)],
        scratch_shapes=[chunk_maps] * 4 + [rows_saved] * 2 + [pltpu.VMEM((n_chunks, RWKV_CHUNK, LANES), F32)],
        compiler_params=_params(("arbitrary",), vmem),
        name="rwkv_prompt",
    )(r3(r), r3(k), r3(v), r3(lw), r3(a), r3(g), v2(k_k), v2(k_a), v2(r_k), v2(gn_g), v2(gn_b))
    return out.reshape(batch * seq, d), state.reshape(batch, d // hd, hd, hd)


def _rwkv_sample_body(s_ref, r_ref, k_ref, lw_ref, a_ref, v_ref, kk_ref, ka_ref, rk_ref, gng_ref, gnb_ref,
                      y_ref, so_ref):
    r, k, a, v = r_ref[...], k_ref[...], a_ref[...], v_ref[...]
    s = s_ref[...]

    def flipped(x):
        return jnp.swapaxes(jnp.broadcast_to(x, s.shape), 1, 2)

    kk = k * kk_ref[...]
    kk = kk / jnp.maximum(jnp.sqrt(jnp.sum(kk * kk, axis=-1, keepdims=True)), 1e-12)
    k2 = k * (1.0 + (a - 1.0) * ka_ref[...])
    sa = -jnp.sum(s * kk, axis=-1, keepdims=True)
    s = s * jnp.exp(lw_ref[...]) + sa * (kk * a) + flipped(v) * k2
    so_ref[...] = s
    y = flipped(jnp.sum(s * r, axis=-1, keepdims=True))[:, 0:1, :]
    mean = jnp.mean(y, axis=-1, keepdims=True)
    var = jnp.mean(jnp.square(y - mean), axis=-1, keepdims=True)
    y = (y - mean) * lax.rsqrt(var + GN_EPS) * gng_ref[...] + gnb_ref[...]
    bonus = jnp.sum(r * k2 * rk_ref[...], axis=-1, keepdims=True)
    y_ref[...] = y + bonus * v


def _rwkv_sample(state, r, k, v, lw, a, k_k, k_a, r_k, gn_g, gn_b):
    nb, heads, hd, _ = state.shape
    rowv = lambda t: t.reshape(nb, heads, 1, hd)
    rowp = lambda t: t.reshape(heads, 1, hd)
    st = pl.BlockSpec((None, heads, hd, hd), lambda b: (b, 0, 0, 0))
    rv = pl.BlockSpec((None, heads, 1, hd), lambda b: (b, 0, 0, 0))
    rp = pl.BlockSpec((heads, 1, hd), lambda b: (0, 0, 0))
    y, new_state = pl.pallas_call(
        _rwkv_sample_body,
        grid=(nb,),
        in_specs=[st, rv, rv, rv, rv, rv, rp, rp, rp, rp, rp],
        out_specs=[rv, st],
        out_shape=[jax.ShapeDtypeStruct((nb, heads, 1, hd), F32), jax.ShapeDtypeStruct(state.shape, F32)],
        compiler_params=_params(("parallel",), 32 << 20),
        name="rwkv_sample",
    )(state, rowv(r), rowv(k), rowv(lw), rowv(a), rowv(v),
      rowp(k_k), rowp(k_a), rowp(r_k), rowp(gn_g), rowp(gn_b))
    return y.reshape(nb, heads * hd), new_state


def _gate_cast_body(y_ref, g_ref, o_ref):
    o_ref[...] = (y_ref[...] * g_ref[...]).astype(o_ref.dtype)


def _gate_cast(y, g):
    return pl.pallas_call(_gate_cast_body, out_shape=jax.ShapeDtypeStruct(y.shape, BF16), name="gate_cast")(y, g)


def kernel(x_prompt, x_sample, cache_a_k, cache_a_v, state_c_wkv, state_c_shift, p_prompt, p_sample, norm_g, final_norm_g, rel_bias, ab_w_in, ab_w_out, b_w_s, b_b_s, b_ln_g, b_ln_b, c_mu, c_w_r, c_w_k, c_w_v, c_w_g, c_w_o, c_w0, c_w1, c_w2, c_a0, c_a1, c_a2, c_k_k, c_k_a, c_r_k, c_gn_g, c_gn_b, ple_w_proj, ple_w_gate):
    batch, seq, d = x_prompt.shape
    nb = x_sample.shape[0]
    assert x_sample.shape[1] == 1 and seq % (QBLK * max(dil for _, dil in DILATION_PATTERNS)) == 0
    assert norm_g.shape[0] == 2, "layer pattern implemented for depth 2: one attention+gMLP layer, one RWKV-7 layer"
    a_heads = cache_a_k.shape[3]
    c_heads = state_c_wkv.shape[2]
    m = batch * seq
    hp = x_prompt.reshape(m, d)
    hs = x_sample.reshape(nb, d)
    pp = p_prompt.reshape(p_prompt.shape[0], m, -1)
    ps = p_sample.reshape(p_sample.shape[0], nb, -1)
    w_gate = ple_w_gate.astype(BF16)
    w_proj = ple_w_proj.astype(BF16)

    w_in = ab_w_in[0].astype(BF16)
    w_out = ab_w_out[0].astype(BF16)
    bias_prompt = _prompt_bias(rel_bias)
    bias_cache, bias_own = _sample_bias(rel_bias)

    q, k, v, ga, ob = _ab_in_proj(hp, norm_g[0], w_in, b_ln_g[0], b_ln_b[0], tm=256, w_s=b_w_s[0], b_s=b_b_s[0])
    oa = _attn_prompt(q, k, v, ga, bias_prompt, batch, seq)
    hp = _out_ple([oa, ob], hp, pp, 0, w_out, w_gate, w_proj, None, tm=256)
    a_k_p = k.reshape(1, batch, seq, a_heads, A_HEAD_DIM)
    a_v_p = v.reshape(1, batch, seq, a_heads, A_HEAD_DIM)

    qs, ks, vs, gas, ubs, vns, gbs = _ab_in_proj(hs, norm_g[0], w_in, b_ln_g[0], b_ln_b[0], tm=nb)
    oas = _attn_sample(qs, ks, vs, gas, cache_a_k[0], cache_a_v[0], bias_cache, bias_own)
    obs = _gmlp_sample(vns, ubs, gbs, b_w_s[0], b_b_s[0])
    hs = _out_ple([oas, obs], hs, ps, 0, w_out, w_gate, w_proj, None, tm=nb)
    a_k_s = ks.reshape(1, nb, 1, a_heads, A_HEAD_DIM)
    a_v_s = vs.reshape(1, nb, 1, a_heads, A_HEAD_DIM)
    b_v_s = vns.reshape(1, nb, 1, -1)

    mu = c_mu[0]
    w_o = c_w_o[0].astype(BF16)
    rk_flat = c_r_k[0].reshape(-1)
    lora = (c_w1[0], c_w2[0], c_w0[0], c_a1[0], c_a2[0], c_a0[0])

    def projections(h, prev, tm, seq_rows):
        r, kx, vx = _rwkv_proj(h, prev, norm_g[1], mu, [c_w_r[0], c_w_k[0], c_w_v[0]], (0, 2, 3), (None, None, None),
                               None, None, tm, seq_rows)
        g, lw, a = _rwkv_proj(h, prev, norm_g[1], mu, [c_w_g[0]], (5,), (_silu,), lora, (1, 4), tm, seq_rows)
        return r, kx, vx, g, lw, a

    r, kx, vx, g, lw, a = projections(hp, None, 256, seq)
    yg, s_p = _rwkv_prompt(r, kx, vx, lw, a, g, c_k_k[0], c_k_a[0], rk_flat, c_gn_g[0], c_gn_b[0], batch, seq)
    y_prompt = _out_ple([yg], hp, pp, 1, w_o, w_gate, w_proj, final_norm_g, tm=256)
    last_rows = hp.reshape(batch, seq, d)[:, seq - SUBLANES:, :].reshape(batch * SUBLANES, d)
    sh_p = _norm_rows(last_rows, norm_g[1]).reshape(batch, SUBLANES, d)[:, SUBLANES - 1]

    rs, kxs, vxs, gs, lws, a_s = projections(hs, state_c_shift[0], nb, 1)
    ys, s_s = _rwkv_sample(state_c_wkv[0], rs, kxs, vxs, lws, a_s, c_k_k[0], c_k_a[0], rk_flat, c_gn_g[0], c_gn_b[0])
    y_sample = _out_ple([_gate_cast(ys, gs)], hs, ps, 1, w_o, w_gate, w_proj, final_norm_g, tm=nb)
    sh_s = _norm_rows(hs, norm_g[1])

    return (y_prompt.reshape(batch, seq, d), y_sample.reshape(nb, 1, d), a_k_p, a_v_p, a_k_s, a_v_s, b_v_s,
            s_p[None], sh_p[None], s_s[None], sh_s[None])


def _norm_rows_body(x_ref, g_ref, o_ref):
    o_ref[...] = _rms_norm(x_ref[...], g_ref[...])


def _norm_rows(x, g):
    return pl.pallas_call(_norm_rows_body, out_shape=jax.ShapeDtypeStruct(x.shape, F32), name="norm_rows")(
        x, g.reshape(1, -1))
```

```python
import functools
import math

import jax
import jax.numpy as jnp
from jax import lax
from jax.experimental import pallas as pl
from jax.experimental.pallas import tpu as pltpu

F32 = jnp.float32
BF16 = jnp.bfloat16

LANES = 128
SUBLANES = 8
VMEM_BUDGET_BYTES = 56 * 1024 * 1024

A_HEAD_DIM = 128
DILATION_PATTERNS = ((128, 1), (512, 4), (2048, 16))
QBLK = 128
ATTN_GROUP = 8
REL_BUCKETS = 32
REL_MAX_DIST = 2048
CHUNK = 128
C_HEAD_DIM = 64
TICK_EVERY = 12
RWKV_CHUNK = 64
RMS_EPS = 1e-6
LN_EPS = 1e-5
GN_EPS = 64e-5
NEG_INF = -1e30


def _params(semantics, vmem_bytes):
    return pltpu.CompilerParams(dimension_semantics=semantics, vmem_limit_bytes=int(vmem_bytes))


def _resident(shape, index_map):
    return pl.BlockSpec(shape, index_map, pipeline_mode=pl.Buffered(1))


def _bf16_terms(x, n):
    if x.dtype == BF16 or n == 1:
        return [x.astype(BF16)]
    terms, rest = [], x
    for _ in range(n):
        terms.append(rest.astype(BF16))
        rest = rest - terms[-1].astype(F32)
    return terms


def _dot_dims(a, b, dims, terms):
    a_terms = _bf16_terms(a, terms[0])
    b_terms = _bf16_terms(b, terms[1])
    out = None
    for i, at in enumerate(a_terms):
        for j, bt in enumerate(b_terms):
            if i + j < max(len(a_terms), len(b_terms)):
                part = lax.dot_general(at, bt, (dims, ((), ())), preferred_element_type=F32)
                out = part if out is None else out + part
    return out


def _dot(a, b, terms=(1, 1)):
    return _dot_dims(a, b, ((1,), (0,)), terms)


def _dot_nt(a, b, terms=(1, 1)):
    return _dot_dims(a, b, ((1,), (1,)), terms)


def _dot_tn(a, b, terms=(1, 1)):
    return _dot_dims(a, b, ((0,), (0,)), terms)


def _rms_norm(x, g):
    return x * lax.rsqrt(jnp.mean(x * x, axis=-1, keepdims=True) + RMS_EPS) * g


def _layer_norm(x, g, b):
    mu = jnp.mean(x, axis=-1, keepdims=True)
    var = jnp.mean(jnp.square(x - mu), axis=-1, keepdims=True)
    return (x - mu) * lax.rsqrt(var + LN_EPS) * g + b


def _silu(x):
    return x * jax.nn.sigmoid(x)


def _ab_in_body(x_ref, g_ref, w_ref, lng_ref, lnb_ref, *rest, mix_chunks):
    xn = _rms_norm(x_ref[...], g_ref[...]).astype(BF16)
    if mix_chunks:
        ws_ref, bs_ref, q_ref, k_ref, v_ref, ga_ref, ob_ref = rest
    else:
        q_ref, k_ref, v_ref, ga_ref, ub_ref, vn_ref, gb_ref = rest
    width = q_ref.shape[1]
    column_group = lambda idx: _dot(xn, w_ref[:, idx * width:(idx + 1) * width])
    q_ref[...] = column_group(0)
    k_ref[...] = column_group(1)
    v_ref[...] = column_group(2)
    ga_ref[...] = _silu(column_group(3))
    ub = jax.nn.gelu(column_group(4))
    vn = _layer_norm(jax.nn.gelu(column_group(5)), lng_ref[...], lnb_ref[...])
    gb = _silu(column_group(6))
    if not mix_chunks:
        ub_ref[...], vn_ref[...], gb_ref[...] = ub, vn, gb
        return
    ii = lax.broadcasted_iota(jnp.int32, (CHUNK, CHUNK), 0)
    jj = lax.broadcasted_iota(jnp.int32, (CHUNK, CHUNK), 1)
    vn = vn.astype(BF16)
    for grp in range(ws_ref.shape[0]):
        w = jnp.where(jj <= ii, ws_ref[grp], 0.0).astype(BF16)
        cols = slice(grp * LANES, (grp + 1) * LANES)
        for c in range(xn.shape[0] // CHUNK):
            rows = slice(c * CHUNK, (c + 1) * CHUNK)
            s = _dot(w, vn[rows, cols]) + bs_ref[:, cols]
            ob_ref[rows, cols] = (ub[rows, cols] * s * gb[rows, cols]).astype(ob_ref.dtype)


def _ab_in_proj(x, norm_g, w_in_bf16, ln_g, ln_b, tm, w_s=None, b_s=None):
    m, d = x.shape
    width = w_in_bf16.shape[1] // 7
    out = jax.ShapeDtypeStruct((m, width), F32)
    row = lambda i: (i, 0)
    fixed = lambda i: (0, 0)
    mix_chunks = w_s is not None
    in_specs = [pl.BlockSpec((tm, d), row), pl.BlockSpec((1, d), fixed), _resident(w_in_bf16.shape, fixed),
                pl.BlockSpec((1, width), fixed), pl.BlockSpec((1, width), fixed)]
    args = [x, norm_g.reshape(1, d), w_in_bf16, ln_g.reshape(1, width), ln_b.reshape(1, width)]
    if mix_chunks:
        assert tm % CHUNK == 0 and width == w_s.shape[0] * LANES
        groups = w_s.shape[0]
        in_specs += [pl.BlockSpec((groups, CHUNK, CHUNK), lambda i: (0, 0, 0)), pl.BlockSpec((CHUNK, width), fixed)]
        args += [w_s, jnp.repeat(b_s.T, width // groups, axis=1)]
        out_shape = [out] * 4 + [jax.ShapeDtypeStruct((m, width), BF16)]
    else:
        out_shape = [out] * 7
    vmem = w_in_bf16.size * 2 + 2 * tm * d * 4 + len(out_shape) * 2 * tm * width * 4 + tm * d * 2 + 10 * tm * width * 4
    return pl.pallas_call(
        functools.partial(_ab_in_body, mix_chunks=mix_chunks),
        grid=(m // tm,),
        in_specs=in_specs,
        out_specs=[pl.BlockSpec((tm, width), row)] * len(out_shape),
        out_shape=out_shape,
        compiler_params=_params(("parallel",), vmem),
        name="ab_in_proj",
    )(*args)


def _t5_bucket(dist):
    n_exact = REL_BUCKETS // 2
    d = jnp.maximum(dist, 1).astype(F32)
    log_b = n_exact + (jnp.log(d / n_exact) / math.log(REL_MAX_DIST / n_exact) * (REL_BUCKETS - n_exact)).astype(jnp.int32)
    return jnp.where(dist < n_exact, dist, jnp.minimum(log_b, REL_BUCKETS - 1))


def _bias_at(rel_bias, dist):
    one_hot = jax.nn.one_hot(_t5_bucket(dist), REL_BUCKETS, dtype=F32)
    return jnp.einsum("...k,kh->...h", one_hot, rel_bias.astype(F32), precision=lax.Precision.HIGHEST)


def _prompt_bias(rel_bias):
    i = jnp.arange(QBLK)[:, None]
    j = jnp.arange(2 * QBLK)[None, :]
    steps = QBLK + i - j
    tables = []
    for window, dil in DILATION_PATTERNS:
        band = (steps >= 0) & (steps <= window // dil)
        bias = jnp.moveaxis(_bias_at(rel_bias, jnp.clip(steps, 0) * dil), -1, 0)
        tables.append(jnp.where(band[None], bias, NEG_INF))
    return jnp.stack(tables)


def _sample_bias(rel_bias):
    back = QBLK - jnp.arange(QBLK)
    cache = jnp.stack([_bias_at(rel_bias, back * dil) for _, dil in DILATION_PATTERNS])
    own = _bias_at(rel_bias, jnp.zeros((), jnp.int32))
    return cache, own


def _attn_prompt_body(q_ref, k_ref, v_ref, gate_ref, bias_ref, o_ref, o_scr, lse_scr, *, seq):
    scale = A_HEAD_DIM ** -0.5

    def blocks(p, dil, starts, with_prev):
        ds = lambda st: pl.ds(st, QBLK, stride=dil) if dil > 1 else pl.ds(st, QBLK)
        rows = [ds(st) for st in starts]
        q = [q_ref[rw, :].astype(BF16) for rw in rows]
        k = [k_ref[rw, :] for rw in rows]
        v = [v_ref[rw, :] for rw in rows]
        if with_prev:
            prev = [ds(st - QBLK * dil) for st in starts]
            k = [jnp.concatenate([k_ref[pv, :], x], axis=0) for pv, x in zip(prev, k)]
            v = [jnp.concatenate([v_ref[pv, :], x], axis=0) for pv, x in zip(prev, v)]
            bias = bias_ref[p]
        else:
            bias = bias_ref[p, :, QBLK:]
        s = [_dot_nt(x, y) * scale + bias for x, y in zip(q, k)]
        m = [jnp.max(x, axis=-1, keepdims=True) for x in s]
        e = [jnp.exp(x - y) for x, y in zip(s, m)]
        l = [jnp.sum(x, axis=-1, keepdims=True) for x in e]
        o = [_dot(x, y) / z for x, y, z in zip(e, v, l)]
        for rw, x, y, z in zip(rows, o, m, l):
            o_scr[p, rw, :] = x
            lse_scr[p, rw, :] = jnp.broadcast_to(y + jnp.log(z), (QBLK, A_HEAD_DIM))

    def widest_group(count):
        return max(g for g in range(1, ATTN_GROUP + 1) if count % g == 0)

    for p, (_, dil) in enumerate(DILATION_PATTERNS):
        nb = seq // (QBLK * dil)
        shift = dil.bit_length() - 1
        g_first = widest_group(dil)

        def first(i, carry, p=p, dil=dil, g=g_first):
            blocks(p, dil, [i * g + u for u in range(g)], False)
            return carry

        lax.fori_loop(0, dil // g_first, first, 0)
        if nb > 1:
            g_rest = widest_group(dil * (nb - 1))

            def rest(i, carry, p=p, dil=dil, shift=shift, g=g_rest):
                idx = [i * g + u for u in range(g)]
                blocks(p, dil, [((x >> shift) + 1) * (QBLK * dil) + (x & (dil - 1)) for x in idx], True)
                return carry

            lax.fori_loop(0, dil * (nb - 1) // g_rest, rest, 0)

    lse = [lse_scr[p] for p in range(len(DILATION_PATTERNS))]
    m = functools.reduce(jnp.maximum, lse)
    e = [jnp.exp(t - m) for t in lse]
    den = functools.reduce(jnp.add, e)
    mix = functools.reduce(jnp.add, [(e[p] / den) * o_scr[p] for p in range(len(e))])
    o_ref[...] = (mix * gate_ref[...]).astype(o_ref.dtype)


def _attn_prompt(q, k, v, gate, bias, batch, seq):
    width = q.shape[1]
    heads = width // A_HEAD_DIM
    n_pat = len(DILATION_PATTERNS)
    blk = pl.BlockSpec((None, seq, A_HEAD_DIM), lambda b, h: (b, 0, h))
    r3 = lambda t: t.reshape(batch, seq, width)
    vmem = 4 * 2 * seq * A_HEAD_DIM * 4 + 2 * seq * A_HEAD_DIM * 2 + 2 * n_pat * seq * A_HEAD_DIM * 4 \
        + 2 * n_pat * QBLK * 2 * QBLK * 4 + (8 << 20)
    out = pl.pallas_call(
        functools.partial(_attn_prompt_body, seq=seq),
        grid=(batch, heads),
        in_specs=[blk, blk, blk, blk,
                  pl.BlockSpec((n_pat, None, QBLK, 2 * QBLK), lambda b, h: (0, h, 0, 0))],
        out_specs=blk,
        out_shape=jax.ShapeDtypeStruct((batch, seq, width), BF16),
        scratch_shapes=[pltpu.VMEM((n_pat, seq, A_HEAD_DIM), F32), pltpu.VMEM((n_pat, seq, A_HEAD_DIM), F32)],
        compiler_params=_params(("parallel", "parallel"), vmem),
        name="attn_prompt",
    )(r3(q), r3(k), r3(v), r3(gate), bias)
    return out.reshape(batch * seq, width)


def _attn_sample_body(q_ref, kn_ref, vn_ref, gate_ref, bc_ref, bo_ref, *rest):
    n_pat = len(DILATION_PATTERNS)
    kc_refs, vc_refs, o_ref = rest[:n_pat], rest[n_pat:2 * n_pat], rest[2 * n_pat]
    scale = A_HEAD_DIM ** -0.5
    rnd = lambda t: t.astype(BF16).astype(F32)
    q = rnd(q_ref[...])
    kn = rnd(kn_ref[...])
    vn = rnd(vn_ref[...])
    s_new = jnp.sum(q * kn, axis=-1, keepdims=True) * scale + bo_ref[...]
    outs, lses = [], []
    for p in range(n_pat):
        kc = rnd(kc_refs[p][...])
        vc = rnd(vc_refs[p][...])
        s = jnp.sum(q[None] * kc, axis=-1, keepdims=True) * scale + bc_ref[p]
        m = jnp.maximum(jnp.max(s, axis=0), s_new)
        e = jnp.exp(s - m[None])
        e_new = jnp.exp(s_new - m)
        l = jnp.sum(e, axis=0) + e_new
        outs.append((jnp.sum(rnd(e) * vc, axis=0) + rnd(e_new) * vn) / l)
        lses.append(m + jnp.log(l))
    m = functools.reduce(jnp.maximum, lses)
    e = [jnp.exp(t - m) for t in lses]
    den = functools.reduce(jnp.add, e)
    mix = functools.reduce(jnp.add, [(e[p] / den) * outs[p] for p in range(n_pat)])
    o_ref[...] = (mix * gate_ref[...]).astype(o_ref.dtype)


def _attn_sample(q, k_new, v_new, gate, cache_k, cache_v, bias_cache, bias_own):
    nb, past, heads, hd = cache_k.shape
    n_pat = len(DILATION_PATTERNS)
    row = pl.BlockSpec((None, heads, hd), lambda b: (b, 0, 0))
    r3 = lambda t: t.reshape(nb, heads, hd)
    cache_specs, cache_args = [], []
    for cache in (cache_k, cache_v):
        for window, dil in DILATION_PATTERNS:
            last = past // (QBLK * dil) - 1
            cache_specs.append(pl.BlockSpec((None, QBLK, None, heads, hd), lambda b, last=last: (b, last, 0, 0, 0)))
            cache_args.append(cache.reshape(nb, past // dil, dil, heads, hd))
    vmem = 2 * 2 * n_pat * QBLK * heads * hd * 4 + 8 * QBLK * heads * hd * 4 + (8 << 20)
    out = pl.pallas_call(
        _attn_sample_body,
        grid=(nb,),
        in_specs=[row, row, row, row,
                  pl.BlockSpec((n_pat, QBLK, heads, 1), lambda b: (0, 0, 0, 0)),
                  pl.BlockSpec((heads, 1), lambda b: (0, 0))] + cache_specs,
        out_specs=row,
        out_shape=jax.ShapeDtypeStruct((nb, heads, hd), BF16),
        compiler_params=_params(("parallel",), vmem),
        name="attn_sample",
    )(r3(q), r3(k_new), r3(v_new), r3(gate), bias_cache[..., None], bias_own.reshape(heads, 1), *cache_args)
    return out.reshape(nb, heads * hd)


def _gmlp_sample_body(vn_ref, ub_ref, gb_ref, w0_ref, b0_ref, o_ref):
    s = w0_ref[...].astype(BF16).astype(F32) * vn_ref[...].astype(BF16).astype(F32) + b0_ref[...]
    o_ref[...] = (ub_ref[...] * s * gb_ref[...]).astype(o_ref.dtype)


def _gmlp_sample(vn, ub, gb, w_s, b_s):
    m, width = vn.shape
    groups = w_s.shape[0]
    w0 = jnp.repeat(w_s[:, 0, 0], width // groups).reshape(1, width)
    b0 = jnp.repeat(b_s[:, 0], width // groups).reshape(1, width)
    return pl.pallas_call(
        _gmlp_sample_body,
        out_shape=jax.ShapeDtypeStruct((m, width), BF16),
        name="gmlp_sample",
    )(vn, ub, gb, w0, b0)


def _out_ple_body(*refs, n_a, final_norm):
    a_refs = refs[:n_a]
    x_ref, p_ref, wo_ref, wg_ref, wp_ref = refs[n_a:n_a + 5]
    rest = refs[n_a + 5:]
    h = x_ref[...]
    off = 0
    for a_ref in a_refs:
        ka = a_ref.shape[1]
        h = h + _dot(a_ref[...], wo_ref[off:off + ka, :])
        off += ka
    gate = jax.nn.sigmoid(_dot(h.astype(BF16), wg_ref[...]))
    h = h + gate * _dot(p_ref[...].astype(BF16), wp_ref[...])
    if final_norm:
        fg_ref, o_ref = rest
        o_ref[...] = _rms_norm(h, fg_ref[...])
    else:
        (o_ref,) = rest
        o_ref[...] = h


def _out_ple(a_list, x, p_layers, layer, w_out_bf16, w_gate_layers, w_proj_layers, final_g, tm):
    m, d = x.shape
    pd = p_layers.shape[2]
    row = lambda i: (i, 0)
    fixed = lambda i: (0, 0)
    of_layer = lambda i: (layer, 0, 0)
    in_specs = [pl.BlockSpec((tm, a.shape[1]), row) for a in a_list]
    in_specs += [pl.BlockSpec((tm, d), row), pl.BlockSpec((None, tm, pd), lambda i: (layer, i, 0)),
                 _resident(w_out_bf16.shape, fixed), _resident((None, d, d), of_layer), _resident((None, pd, d), of_layer)]
    args = list(a_list) + [x, p_layers, w_out_bf16, w_gate_layers, w_proj_layers]
    if final_g is not None:
        in_specs.append(pl.BlockSpec((1, d), fixed))
        args.append(final_g.reshape(1, d))
    vmem = (w_out_bf16.size + d * d + pd * d) * 2 + 2 * tm * (2 * d * 4 + pd * 4 + w_out_bf16.shape[0] * 2) \
        + 4 * tm * d * 4 + (6 << 20)
    return pl.pallas_call(
        functools.partial(_out_ple_body, n_a=len(a_list), final_norm=final_g is not None),
        grid=(m // tm,),
        in_specs=in_specs,
        out_specs=pl.BlockSpec((tm, d), row),
        out_shape=jax.ShapeDtypeStruct((m, d), F32),
        compiler_params=_params(("parallel",), vmem),
        name="out_ple",
    )(*args)


def _shifted_inputs(h_ref, prev_ref, g_ref, *, seq_rows, prev_given):
    xn = _rms_norm(h_ref[...], g_ref[...])
    if prev_given:
        return xn, prev_ref[...]
    tm = xn.shape[0]
    before = _rms_norm(prev_ref[...], g_ref[...])[SUBLANES - 1:SUBLANES, :]
    starts_sequence = (pl.program_id(0) * tm) % seq_rows == 0
    before = jnp.where(starts_sequence, 0.0, before)
    first = lax.broadcasted_iota(jnp.int32, xn.shape, 0) == 0
    return xn, jnp.where(first, before, pltpu.roll(xn, 1, 0))


def _rwkv_proj_body(h_ref, prev_ref, g_ref, mu_ref, w_ref, *rest, seq_rows, prev_given, mixes, acts, lora_mixes):
    xn, x_prev = _shifted_inputs(h_ref, prev_ref, g_ref, seq_rows=seq_rows, prev_given=prev_given)
    xx = x_prev - xn
    mixed = lambda idx: (xn + xx * mu_ref[idx:idx + 1, :]).astype(BF16)
    if lora_mixes is not None:
        w1_ref, w2_ref, w0_ref, a1_ref, a2_ref, a0_ref = rest[:6]
        rest = rest[6:]
    for i, (mix, act) in enumerate(zip(mixes, acts)):
        z = _dot(mixed(mix), w_ref[i])
        rest[i][...] = act(z) if act is not None else z
    if lora_mixes is not None:
        lw_ref, a_ref = rest[len(mixes):]
        zw = w0_ref[...] + _dot(jnp.tanh(_dot(mixed(lora_mixes[0]), w1_ref[...])), w2_ref[...])
        lw_ref[...] = -math.exp(-0.5) * jax.nn.sigmoid(zw)
        za = a0_ref[...] + _dot(_dot(mixed(lora_mixes[1]), a1_ref[...]), a2_ref[...])
        a_ref[...] = jax.nn.sigmoid(za)


def _pad_lora(w_down, w_up):
    rank = w_down.shape[1]
    pad = (-rank) % LANES
    return (jnp.pad(w_down, ((0, 0), (0, pad))).astype(BF16), jnp.pad(w_up, ((0, pad), (0, 0))).astype(BF16))


def _rwkv_proj(h, prev, norm_g, mu, weights, mixes, acts, lora, lora_mixes, tm, seq_rows):
    m, d = h.shape
    prev_given = prev is not None
    row = lambda i: (i, 0)
    fixed = lambda i: (0, 0)
    if prev_given:
        specs, args = [pl.BlockSpec((tm, d), row), pl.BlockSpec((tm, d), row)], [h, prev]
    else:
        per = tm // SUBLANES
        specs = [pl.BlockSpec((tm, d), row), pl.BlockSpec((SUBLANES, d), lambda i: (jnp.maximum(i * per - 1, 0), 0))]
        args = [h, h]
    w_stack = jnp.stack(weights).astype(BF16)
    specs += [pl.BlockSpec((1, d), fixed), pl.BlockSpec(mu.shape, fixed), _resident(w_stack.shape, lambda i: (0, 0, 0))]
    args += [norm_g.reshape(1, d), mu, w_stack]
    n_out = len(weights)
    lora_bytes = 0
    if lora is not None:
        w1, w2, w0, a1, a2, a0 = lora
        w1p, w2p = _pad_lora(w1, w2)
        a1p, a2p = _pad_lora(a1, a2)
        rank = w1p.shape[1]
        vec, down, up = pl.BlockSpec((1, d), fixed), pl.BlockSpec((d, rank), fixed), pl.BlockSpec((rank, d), fixed)
        specs += [down, up, vec, down, up, vec]
        args += [w1p, w2p, w0.reshape(1, d), a1p, a2p, a0.reshape(1, d)]
        n_out += 2
        lora_bytes = 2 * 4 * d * rank * 2
    out = jax.ShapeDtypeStruct((m, d), F32)
    vmem = w_stack.size * 2 + lora_bytes + (2 + 2 * n_out + 8) * tm * d * 4 + (2 << 20)
    return pl.pallas_call(
        functools.partial(_rwkv_proj_body, seq_rows=seq_rows, prev_given=prev_given, mixes=mixes, acts=acts,
                          lora_mixes=lora_mixes if lora is not None else None),
        grid=(m // tm,),
        in_specs=specs,
        out_specs=[pl.BlockSpec((tm, d), row)] * n_out,
        out_shape=[out] * n_out,
        compiler_params=_params(("parallel",), vmem),
        name="rwkv_proj",
    )(*args)


def _rwkv_prompt_body(r_ref, k_ref, v_ref, lw_ref, a_ref, g_ref, kk_ref, ka_ref, rk_ref, gng_ref, gnb_ref,
                      o_ref, st_ref, q2_scr, y0_scr, m_scr, h0_scr, bv_scr, gate_scr, hs_scr, *, seq):
    c_len = RWKV_CHUNK
    n_chunks = seq // c_len
    hd = C_HEAD_DIM
    assert c_len == hd and LANES % hd == 0, "time x time and key x value blocks share one lane tiling"
    lane = lax.broadcasted_iota(jnp.int32, (1, LANES), 1)
    head_masks = (lane < hd, lane >= hd)
    ti = lax.broadcasted_iota(jnp.int32, (c_len, c_len), 0)
    si = lax.broadcasted_iota(jnp.int32, (c_len, c_len), 1)
    tri = (ti >= si).astype(F32)
    tp = lax.broadcasted_iota(jnp.int32, (c_len, LANES), 0)
    sp = lax.broadcasted_iota(jnp.int32, (c_len, LANES), 1) % hd
    incl = tp >= sp
    strict = tp > sp
    eye_c = (tp == sp).astype(F32)
    bi = lax.broadcasted_iota(jnp.int32, (LANES, LANES), 0)
    bj = lax.broadcasted_iota(jnp.int32, (LANES, LANES), 1)
    same_head = (bi // hd) == (bj // hd)
    head_ones = same_head.astype(BF16)

    def head_sum(x):
        return _dot(x, head_ones)

    def stacked(z):
        masks = [jnp.concatenate([mk] * (z.shape[1] // LANES), axis=1) for mk in head_masks]
        return jnp.concatenate([jnp.where(mk, z, 0.0) for mk in masks], axis=0)

    def block_diag(x):
        return jnp.where(same_head, jnp.concatenate([x] * len(head_masks), axis=0), 0.0)

    def diag_blocks(x):
        x = jnp.where(same_head, x, 0.0)
        return x[:hd] + x[hd:]

    step = pl.program_id(0)
    cur = lax.rem(step, 2)
    prev = 1 - cur

    @pl.when(step == 0)
    def _():
        for scr in (q2_scr, y0_scr, m_scr, h0_scr, bv_scr, gate_scr):
            scr[1] = jnp.zeros(scr.shape[1:], scr.dtype)

    chunks = list(range(n_chunks))
    rows = [slice(c * c_len, (c + 1) * c_len) for c in chunks]
    state = [jnp.zeros((hd, LANES), F32)]
    carried = [0]

    def tick():
        c = carried[0]
        if c < n_chunks:
            hs_scr[c] = state[0]
            state[0] = _dot(m_scr[prev, c], stacked(state[0]), (2, 2)) + h0_scr[prev, c]
            carried[0] = c + 1

    def stage(fn, *lists):
        out = []
        for i, args in enumerate(zip(*lists)):
            out.append(fn(*args))
            if i % TICK_EVERY == TICK_EVERY - 1:
                tick()
        return out

    r = [r_ref[rw, :] for rw in rows]
    k = [k_ref[rw, :] for rw in rows]
    v = [v_ref[rw, :] for rw in rows]
    a = [a_ref[rw, :] for rw in rows]
    lw = [lw_ref[rw, :] for rw in rows]
    k2 = [x * (1.0 + (y - 1.0) * ka_ref[...]) for x, y in zip(k, a)]
    kk = [x * kk_ref[...] for x in k]
    norm = stage(lambda x: head_sum(x * x), kk)
    cum = stage(lambda x: _dot(tri, x, (1, 2)), lw)
    bonus = stage(lambda x, y: head_sum(x * y * rk_ref[...]), r, k2)
    kk = [x / jnp.maximum(jnp.sqrt(n), 1e-12) for x, n in zip(kk, norm)]
    last = [x[c_len - 1:c_len, :] for x in cum]
    b = [x * y for x, y in zip(kk, a)]
    ekk = [x * jnp.exp(cm - w) for x, cm, w in zip(kk, cum, lw)]
    er = [x * jnp.exp(cm) for x, cm in zip(r, cum)]
    inv = [jnp.exp(-cm) for cm in cum]
    eb = [x * y for x, y in zip(b, inv)]
    ek = [x * y for x, y in zip(k2, inv)]
    to_end = [jnp.exp(ls) * x for ls, x in zip(last, inv)]
    eb_end = [x * y for x, y in zip(b, to_end)]
    ek_end = [x * y for x, y in zip(k2, to_end)]
    lhs = [jnp.concatenate([x, y], axis=0) for x, y in zip(ekk, er)]
    g = stage(lambda x, y, z: _dot_nt(x, jnp.concatenate([stacked(y), stacked(z)], axis=0)), lhs, eb, ek)
    l_b = [jnp.where(strict, x[:c_len, :LANES], 0.0) for x in g]
    a_b = [jnp.where(incl, x[c_len:, :LANES], 0.0) for x in g]
    l_k = [jnp.where(strict, x[:c_len, LANES:], 0.0) for x in g]
    a_k = [jnp.where(incl, x[c_len:, LANES:], 0.0) for x in g]
    lakv = stage(lambda x, y, z: _dot(jnp.concatenate([x, y], axis=0), stacked(z)), l_k, a_k, v)
    lkv = [x[:c_len] for x in lakv]
    akv = [x[c_len:] for x in lakv]
    levels = int(math.log2(c_len))
    power = [-x for x in l_b]
    inv_t = [eye_c + x for x in power]
    power = stage(lambda x: _dot(x, block_diag(x)), power)
    for level in range(1, levels):
        if level < levels - 1:
            both = stage(lambda x, y: _dot(jnp.concatenate([x, y], axis=0), block_diag(x)), power, inv_t)
            power = [x[:c_len] for x in both]
            inv_t = [x + y[c_len:] for x, y in zip(inv_t, both)]
        else:
            inv_t = stage(lambda x, y: y + _dot(y, block_diag(x)), power, inv_t)
    tz = stage(lambda x, y, z: _dot(x, stacked(jnp.concatenate([y, z], axis=1))), inv_t, ekk, lkv)
    qu = [jnp.concatenate([x[:, :LANES], -x[:, LANES:]], axis=1) for x in tz]
    ab = stage(lambda x, y: _dot(x, stacked(y)), a_b, qu)
    ends = [jnp.concatenate([x, y], axis=0) for x, y in zip(eb_end, ek_end)]
    tails = [jnp.concatenate([x, jnp.concatenate([jnp.zeros_like(y), y], axis=1)], axis=0) for x, y in zip(qu, v)]
    bq = stage(_dot_tn, ends, tails)
    for c in chunks:
        q2_scr[cur, c] = er[c] - ab[c][:, :LANES]
        y0_scr[cur, c] = ab[c][:, LANES:] + akv[c]
        m_scr[cur, c] = eye_c * jnp.exp(last[c]) - diag_blocks(bq[c][:, :LANES])
        h0_scr[cur, c] = diag_blocks(bq[c][:, LANES:])
        bv_scr[cur, rows[c], :] = bonus[c] * v[c]
        gate_scr[cur, rows[c], :] = g_ref[rows[c], :]
    while carried[0] < n_chunks:
        tick()
    st_ref[...] = _dot_tn(state[0], (ti == si).astype(F32), (3, 1))

    y = [_dot(q2_scr[prev, c], stacked(hs_scr[c])) + y0_scr[prev, c] for c in chunks]
    mean = [head_sum(x) * (1.0 / hd) for x in y]
    dev = [x - mu for x, mu in zip(y, mean)]
    var = [head_sum(x * x) * (1.0 / hd) for x in dev]
    for rw, dv, vr in zip(rows, dev, var):
        yn = dv * lax.rsqrt(vr + GN_EPS) * gng_ref[...] + gnb_ref[...]
        o_ref[rw, :] = ((yn + bv_scr[prev, rw, :]) * gate_scr[prev, rw, :]).astype(o_ref.dtype)


def _rwkv_prompt(r, k, v, lw, a, g, k_k, k_a, r_k, gn_g, gn_b, batch, seq):
    d = r.shape[1]
    pairs = d // LANES
    n_seq = batch * pairs
    n_chunks = seq // RWKV_CHUNK
    hd = C_HEAD_DIM
    this = lambda s: jnp.minimum(s, n_seq - 1)
    last = lambda s: jnp.maximum(s - 1, 0)
    blk = pl.BlockSpec((None, seq, LANES), lambda s: (this(s) // pairs, 0, this(s) % pairs))
    vec = pl.BlockSpec((1, LANES), lambda s: (0, this(s) % pairs))
    vec_last = pl.BlockSpec((1, LANES), lambda s: (0, last(s) % pairs))
    r3 = lambda t: t.reshape(batch, seq, d)
    v2 = lambda t: t.reshape(1, d)
    chunk_maps = pltpu.VMEM((2, n_chunks, RWKV_CHUNK, LANES), F32)
    rows_saved = pltpu.VMEM((2, seq, LANES), F32)
    vmem = 2 * 6 * seq * LANES * 4 + 2 * seq * LANES * 2 + (4 * 2 + 2 * 2 + 1) * seq * LANES * 4 + (24 << 20)
    out, state = pl.pallas_call(
        functools.partial(_rwkv_prompt_body, seq=seq),
        grid=(n_seq + 1,),
        in_specs=[blk] * 6 + [vec] * 3 + [vec_last] * 2,
        out_specs=[pl.BlockSpec((None, seq, LANES), lambda s: (last(s) // pairs, 0, last(s) % pairs)),
                   pl.BlockSpec((None, None, LANES, hd), lambda s: (last(s) // pairs, last(s) % pairs, 0, 0))],
        out_shape=[jax.ShapeDtypeStruct((batch, seq, d), BF16),
                   jax.ShapeDtypeStruct((batch, pairs, LANES, hd), F32)],
        scratch_shapes=[chunk_maps] * 4 + [rows_saved] * 2 + [pltpu.VMEM((n_chunks, RWKV_CHUNK, LANES), F32)],
        compiler_params=_params(("arbitrary",), vmem),
        name="rwkv_prompt",
    )(r3(r), r3(k), r3(v), r3(lw), r3(a), r3(g), v2(k_k), v2(k_a), v2(r_k), v2(gn_g), v2(gn_b))
    return out.reshape(batch * seq, d), state.reshape(batch, d // hd, hd, hd)


def _rwkv_sample_body(s_ref, r_ref, k_ref, lw_ref, a_ref, v_ref, kk_ref, ka_ref, rk_ref, gng_ref, gnb_ref,
                      y_ref, so_ref):
    r, k, a, v = r_ref[...], k_ref[...], a_ref[...], v_ref[...]
    s = s_ref[...]

    def flipped(x):
        return jnp.swapaxes(jnp.broadcast_to(x, s.shape), 1, 2)

    kk = k * kk_ref[...]
    kk = kk / jnp.maximum(jnp.sqrt(jnp.sum(kk * kk, axis=-1, keepdims=True)), 1e-12)
    k2 = k * (1.0 + (a - 1.0) * ka_ref[...])
    sa = -jnp.sum(s * kk, axis=-1, keepdims=True)
    s = s * jnp.exp(lw_ref[...]) + sa * (kk * a) + flipped(v) * k2
    so_ref[...] = s
    y = flipped(jnp.sum(s * r, axis=-1, keepdims=True))[:, 0:1, :]
    mean = jnp.mean(y, axis=-1, keepdims=True)
    var = jnp.mean(jnp.square(y - mean), axis=-1, keepdims=True)
    y = (y - mean) * lax.rsqrt(var + GN_EPS) * gng_ref[...] + gnb_ref[...]
    bonus = jnp.sum(r * k2 * rk_ref[...], axis=-1, keepdims=True)
    y_ref[...] = y + bonus * v


def _rwkv_sample(state, r, k, v, lw, a, k_k, k_a, r_k, gn_g, gn_b):
    nb, heads, hd, _ = state.shape
    rowv = lambda t: t.reshape(nb, heads, 1, hd)
    rowp = lambda t: t.reshape(heads, 1, hd)
    st = pl.BlockSpec((None, heads, hd, hd), lambda b: (b, 0, 0, 0))
    rv = pl.BlockSpec((None, heads, 1, hd), lambda b: (b, 0, 0, 0))
    rp = pl.BlockSpec((heads, 1, hd), lambda b: (0, 0, 0))
    y, new_state = pl.pallas_call(
        _rwkv_sample_body,
        grid=(nb,),
        in_specs=[st, rv, rv, rv, rv, rv, rp, rp, rp, rp, rp],
        out_specs=[rv, st],
        out_shape=[jax.ShapeDtypeStruct((nb, heads, 1, hd), F32), jax.ShapeDtypeStruct(state.shape, F32)],
        compiler_params=_params(("parallel",), 32 << 20),
        name="rwkv_sample",
    )(state, rowv(r), rowv(k), rowv(lw), rowv(a), rowv(v),
      rowp(k_k), rowp(k_a), rowp(r_k), rowp(gn_g), rowp(gn_b))
    return y.reshape(nb, heads * hd), new_state


def _gate_cast_body(y_ref, g_ref, o_ref):
    o_ref[...] = (y_ref[...] * g_ref[...]).astype(o_ref.dtype)


def _gate_cast(y, g):
    return pl.pallas_call(_gate_cast_body, out_shape=jax.ShapeDtypeStruct(y.shape, BF16), name="gate_cast")(y, g)


def kernel(x_prompt, x_sample, cache_a_k, cache_a_v, state_c_wkv, state_c_shift, p_prompt, p_sample, norm_g, final_norm_g, rel_bias, ab_w_in, ab_w_out, b_w_s, b_b_s, b_ln_g, b_ln_b, c_mu, c_w_r, c_w_k, c_w_v, c_w_g, c_w_o, c_w0, c_w1, c_w2, c_a0, c_a1, c_a2, c_k_k, c_k_a, c_r_k, c_gn_g, c_gn_b, ple_w_proj, ple_w_gate):
    batch, seq, d = x_prompt.shape
    nb = x_sample.shape[0]
    assert x_sample.shape[1] == 1 and seq % (QBLK * max(dil for _, dil in DILATION_PATTERNS)) == 0
    assert norm_g.shape[0] == 2, "layer pattern implemented for depth 2: one attention+gMLP layer, one RWKV-7 layer"
    a_heads = cache_a_k.shape[3]
    c_heads = state_c_wkv.shape[2]
    m = batch * seq
    hp = x_prompt.reshape(m, d)
    hs = x_sample.reshape(nb, d)
    pp = p_prompt.reshape(p_prompt.shape[0], m, -1)
    ps = p_sample.reshape(p_sample.shape[0], nb, -1)
    w_gate = ple_w_gate.astype(BF16)
    w_proj = ple_w_proj.astype(BF16)

    w_in = ab_w_in[0].astype(BF16)
    w_out = ab_w_out[0].astype(BF16)
    bias_prompt = _prompt_bias(rel_bias)
    bias_cache, bias_own = _sample_bias(rel_bias)

    q, k, v, ga, ob = _ab_in_proj(hp, norm_g[0], w_in, b_ln_g[0], b_ln_b[0], tm=256, w_s=b_w_s[0], b_s=b_b_s[0])
    oa = _attn_prompt(q, k, v, ga, bias_prompt, batch, seq)
    hp = _out_ple([oa, ob], hp, pp, 0, w_out, w_gate, w_proj, None, tm=256)
    a_k_p = k.reshape(1, batch, seq, a_heads, A_HEAD_DIM)
    a_v_p = v.reshape(1, batch, seq, a_heads, A_HEAD_DIM)

    qs, ks, vs, gas, ubs, vns, gbs = _ab_in_proj(hs, norm_g[0], w_in, b_ln_g[0], b_ln_b[0], tm=nb)
    oas = _attn_sample(qs, ks, vs, gas, cache_a_k[0], cache_a_v[0], bias_cache, bias_own)
    obs = _gmlp_sample(vns, ubs, gbs, b_w_s[0], b_b_s[0])
    hs = _out_ple([oas, obs], hs, ps, 0, w_out, w_gate, w_proj, None, tm=nb)
    a_k_s = ks.reshape(1, nb, 1, a_heads, A_HEAD_DIM)
    a_v_s = vs.reshape(1, nb, 1, a_heads, A_HEAD_DIM)
    b_v_s = vns.reshape(1, nb, 1, -1)

    mu = c_mu[0]
    w_o = c_w_o[0].astype(BF16)
    rk_flat = c_r_k[0].reshape(-1)
    lora = (c_w1[0], c_w2[0], c_w0[0], c_a1[0], c_a2[0], c_a0[0])

    def projections(h, prev, tm, seq_rows):
        r, kx, vx = _rwkv_proj(h, prev, norm_g[1], mu, [c_w_r[0], c_w_k[0], c_w_v[0]], (0, 2, 3), (None, None, None),
                               None, None, tm, seq_rows)
        g, lw, a = _rwkv_proj(h, prev, norm_g[1], mu, [c_w_g[0]], (5,), (_silu,), lora, (1, 4), tm, seq_rows)
        return r, kx, vx, g, lw, a

    r, kx, vx, g, lw, a = projections(hp, None, 256, seq)
    yg, s_p = _rwkv_prompt(r, kx, vx, lw, a, g, c_k_k[0], c_k_a[0], rk_flat, c_gn_g[0], c_gn_b[0], batch, seq)
    y_prompt = _out_ple([yg], hp, pp, 1, w_o, w_gate, w_proj, final_norm_g, tm=256)
    last_rows = hp.reshape(batch, seq, d)[:, seq - SUBLANES:, :].reshape(batch * SUBLANES, d)
    sh_p = _norm_rows(last_rows, norm_g[1]).reshape(batch, SUBLANES, d)[:, SUBLANES - 1]

    rs, kxs, vxs, gs, lws, a_s = projections(hs, state_c_shift[0], nb, 1)
    ys, s_s = _rwkv_sample(state_c_wkv[0], rs, kxs, vxs, lws, a_s, c_k_k[0], c_k_a[0], rk_flat, c_gn_g[0], c_gn_b[0])
    y_sample = _out_ple([_gate_cast(ys, gs)], hs, ps, 1, w_o, w_gate, w_proj, final_norm_g, tm=nb)
    sh_s = _norm_rows(hs, norm_g[1])

    return (y_prompt.reshape(batch, seq, d), y_sample.reshape(nb, 1, d), a_k_p, a_v_p, a_k_s, a_v_s, b_v_s,
            s_p[None], sh_p[None], s_s[None], sh_s[None])


def _norm_rows_body(x_ref, g_ref, o_ref):
    o_ref[...] = _rms_norm(x_ref[...], g_ref[...])


def _norm_rows(x, g):
    return pl.pallas_call(_norm_rows_body, out_shape=jax.ShapeDtypeStruct(x.shape, F32), name="norm_rows")(
        x, g.reshape(1, -1))
```

```python
import functools
import math

import jax
import jax.numpy as jnp
from jax import lax
from jax.experimental import pallas as pl
from jax.experimental.pallas import tpu as pltpu

F32 = jnp.float32
BF16 = jnp.bfloat16

LANES = 128
SUBLANES = 8
VMEM_BUDGET_BYTES = 56 * 1024 * 1024

A_HEAD_DIM = 128
DILATION_PATTERNS = ((128, 1), (512, 4), (2048, 16))
QBLK = 128
ATTN_GROUP = 16
REL_BUCKETS = 32
REL_MAX_DIST = 2048
CHUNK = 128
C_HEAD_DIM = 64
TICK_EVERY = 12
RWKV_CHUNK = 64
RMS_EPS = 1e-6
LN_EPS = 1e-5
GN_EPS = 64e-5
NEG_INF = -1e30


def _params(semantics, vmem_bytes):
    return pltpu.CompilerParams(dimension_semantics=semantics, vmem_limit_bytes=int(vmem_bytes))


def _resident(shape, index_map):
    return pl.BlockSpec(shape, index_map, pipeline_mode=pl.Buffered(1))


def _bf16_terms(x, n):
    if x.dtype == BF16 or n == 1:
        return [x.astype(BF16)]
    terms, rest = [], x
    for _ in range(n):
        terms.append(rest.astype(BF16))
        rest = rest - terms[-1].astype(F32)
    return terms


def _dot_dims(a, b, dims, terms):
    a_terms = _bf16_terms(a, terms[0])
    b_terms = _bf16_terms(b, terms[1])
    out = None
    for i, at in enumerate(a_terms):
        for j, bt in enumerate(b_terms):
            if i + j < max(len(a_terms), len(b_terms)):
                part = lax.dot_general(at, bt, (dims, ((), ())), preferred_element_type=F32)
                out = part if out is None else out + part
    return out


def _dot(a, b, terms=(1, 1)):
    return _dot_dims(a, b, ((1,), (0,)), terms)


def _dot_nt(a, b, terms=(1, 1)):
    return _dot_dims(a, b, ((1,), (1,)), terms)


def _dot_tn(a, b, terms=(1, 1)):
    return _dot_dims(a, b, ((0,), (0,)), terms)


def _rms_norm(x, g):
    return x * lax.rsqrt(jnp.mean(x * x, axis=-1, keepdims=True) + RMS_EPS) * g


def _layer_norm(x, g, b):
    mu = jnp.mean(x, axis=-1, keepdims=True)
    var = jnp.mean(jnp.square(x - mu), axis=-1, keepdims=True)
    return (x - mu) * lax.rsqrt(var + LN_EPS) * g + b


def _sigmoid(x):
    return 0.5 * jnp.tanh(0.5 * x) + 0.5


def _silu(x):
    return x * _sigmoid(x)


def _ab_in_body(x_ref, g_ref, w_ref, lng_ref, lnb_ref, *rest, mix_chunks):
    xn = _rms_norm(x_ref[...], g_ref[...]).astype(BF16)
    if mix_chunks:
        ws_ref, bs_ref, q_ref, k_ref, v_ref, ga_ref, ob_ref = rest
    else:
        q_ref, k_ref, v_ref, ga_ref, ub_ref, vn_ref, gb_ref = rest
    width = q_ref.shape[1]
    column_group = lambda idx: _dot(xn, w_ref[:, idx * width:(idx + 1) * width])
    q_ref[...] = column_group(0)
    k_ref[...] = column_group(1)
    v_ref[...] = column_group(2)
    ga_ref[...] = _silu(column_group(3))
    ub = jax.nn.gelu(column_group(4))
    vn = _layer_norm(jax.nn.gelu(column_group(5)), lng_ref[...], lnb_ref[...])
    gb = _silu(column_group(6))
    if not mix_chunks:
        ub_ref[...], vn_ref[...], gb_ref[...] = ub, vn, gb
        return
    ii = lax.broadcasted_iota(jnp.int32, (CHUNK, CHUNK), 0)
    jj = lax.broadcasted_iota(jnp.int32, (CHUNK, CHUNK), 1)
    vn = vn.astype(BF16)
    for grp in range(ws_ref.shape[0]):
        w = jnp.where(jj <= ii, ws_ref[grp], 0.0).astype(BF16)
        cols = slice(grp * LANES, (grp + 1) * LANES)
        for c in range(xn.shape[0] // CHUNK):
            rows = slice(c * CHUNK, (c + 1) * CHUNK)
            s = _dot(w, vn[rows, cols]) + bs_ref[:, cols]
            ob_ref[rows, cols] = (ub[rows, cols] * s * gb[rows, cols]).astype(ob_ref.dtype)


def _ab_in_proj(x, norm_g, w_in_bf16, ln_g, ln_b, tm, w_s=None, b_s=None):
    m, d = x.shape
    width = w_in_bf16.shape[1] // 7
    out = jax.ShapeDtypeStruct((m, width), F32)
    row = lambda i: (i, 0)
    fixed = lambda i: (0, 0)
    mix_chunks = w_s is not None
    in_specs = [pl.BlockSpec((tm, d), row), pl.BlockSpec((1, d), fixed), _resident(w_in_bf16.shape, fixed),
                pl.BlockSpec((1, width), fixed), pl.BlockSpec((1, width), fixed)]
    args = [x, norm_g.reshape(1, d), w_in_bf16, ln_g.reshape(1, width), ln_b.reshape(1, width)]
    if mix_chunks:
        assert tm % CHUNK == 0 and width == w_s.shape[0] * LANES
        groups = w_s.shape[0]
        in_specs += [pl.BlockSpec((groups, CHUNK, CHUNK), lambda i: (0, 0, 0)), pl.BlockSpec((CHUNK, width), fixed)]
        args += [w_s, jnp.repeat(b_s.T, width // groups, axis=1)]
        out_shape = [out] * 4 + [jax.ShapeDtypeStruct((m, width), BF16)]
    else:
        out_shape = [out] * 7
    vmem = w_in_bf16.size * 2 + 2 * tm * d * 4 + len(out_shape) * 2 * tm * width * 4 + tm * d * 2 + 10 * tm * width * 4
    return pl.pallas_call(
        functools.partial(_ab_in_body, mix_chunks=mix_chunks),
        grid=(m // tm,),
        in_specs=in_specs,
        out_specs=[pl.BlockSpec((tm, width), row)] * len(out_shape),
        out_shape=out_shape,
        compiler_params=_params(("parallel",), vmem),
        name="ab_in_proj",
    )(*args)


def _t5_bucket(dist):
    n_exact = REL_BUCKETS // 2
    d = jnp.maximum(dist, 1).astype(F32)
    log_b = n_exact + (jnp.log(d / n_exact) / math.log(REL_MAX_DIST / n_exact) * (REL_BUCKETS - n_exact)).astype(jnp.int32)
    return jnp.where(dist < n_exact, dist, jnp.minimum(log_b, REL_BUCKETS - 1))


def _bias_at(rel_bias, dist):
    one_hot = jax.nn.one_hot(_t5_bucket(dist), REL_BUCKETS, dtype=F32)
    return jnp.einsum("...k,kh->...h", one_hot, rel_bias.astype(F32), precision=lax.Precision.HIGHEST)


def _prompt_bias(rel_bias):
    i = jnp.arange(QBLK)[:, None]
    j = jnp.arange(2 * QBLK)[None, :]
    steps = QBLK + i - j
    tables = []
    for window, dil in DILATION_PATTERNS:
        band = (steps >= 0) & (steps <= window // dil)
        bias = jnp.moveaxis(_bias_at(rel_bias, jnp.clip(steps, 0) * dil), -1, 0)
        tables.append(jnp.where(band[None], bias, NEG_INF))
    return jnp.stack(tables)


def _sample_bias(rel_bias):
    back = QBLK - jnp.arange(QBLK)
    cache = jnp.stack([_bias_at(rel_bias, back * dil) for _, dil in DILATION_PATTERNS])
    own = _bias_at(rel_bias, jnp.zeros((), jnp.int32))
    return cache, own


def _attn_prompt_body(q_ref, k_ref, v_ref, gate_ref, bias_ref, o_ref, o_scr, lse_scr, *, seq):
    scale = A_HEAD_DIM ** -0.5

    def blocks(p, dil, starts, with_prev):
        ds = lambda st: pl.ds(st, QBLK, stride=dil) if dil > 1 else pl.ds(st, QBLK)
        rows = [ds(st) for st in starts]
        q = [q_ref[rw, :].astype(BF16) for rw in rows]
        k = [k_ref[rw, :] for rw in rows]
        v = [v_ref[rw, :] for rw in rows]
        if with_prev:
            prev = [ds(st - QBLK * dil) for st in starts]
            k = [jnp.concatenate([k_ref[pv, :], x], axis=0) for pv, x in zip(prev, k)]
            v = [jnp.concatenate([v_ref[pv, :], x], axis=0) for pv, x in zip(prev, v)]
            bias = bias_ref[p]
        else:
            bias = bias_ref[p, :, QBLK:]
        s = [_dot_nt(x, y) * scale + bias for x, y in zip(q, k)]
        m = [jnp.max(x, axis=-1, keepdims=True) for x in s]
        e = [jnp.exp(x - y) for x, y in zip(s, m)]
        l = [jnp.sum(x, axis=-1, keepdims=True) for x in e]
        o = [_dot(x, y) / z for x, y, z in zip(e, v, l)]
        for rw, x, y, z in zip(rows, o, m, l):
            o_scr[p, rw, :] = x
            lse_scr[p, rw, :] = jnp.broadcast_to(y + jnp.log(z), (QBLK, A_HEAD_DIM))

    def widest_group(count):
        return max(g for g in range(1, ATTN_GROUP + 1) if count % g == 0)

    for p, (_, dil) in enumerate(DILATION_PATTERNS):
        nb = seq // (QBLK * dil)
        shift = dil.bit_length() - 1
        g_first = widest_group(dil)

        def first(i, carry, p=p, dil=dil, g=g_first):
            blocks(p, dil, [i * g + u for u in range(g)], False)
            return carry

        lax.fori_loop(0, dil // g_first, first, 0)
        if nb > 1:
            g_rest = widest_group(dil * (nb - 1))

            def rest(i, carry, p=p, dil=dil, shift=shift, g=g_rest):
                idx = [i * g + u for u in range(g)]
                blocks(p, dil, [((x >> shift) + 1) * (QBLK * dil) + (x & (dil - 1)) for x in idx], True)
                return carry

            lax.fori_loop(0, dil * (nb - 1) // g_rest, rest, 0)

    lse = [lse_scr[p] for p in range(len(DILATION_PATTERNS))]
    m = functools.reduce(jnp.maximum, lse)
    e = [jnp.exp(t - m) for t in lse]
    den = functools.reduce(jnp.add, e)
    mix = functools.reduce(jnp.add, [(e[p] / den) * o_scr[p] for p in range(len(e))])
    o_ref[...] = (mix * gate_ref[...]).astype(o_ref.dtype)


def _attn_prompt(q, k, v, gate, bias, batch, seq):
    width = q.shape[1]
    heads = width // A_HEAD_DIM
    n_pat = len(DILATION_PATTERNS)
    blk = pl.BlockSpec((None, seq, A_HEAD_DIM), lambda b, h: (b, 0, h))
    r3 = lambda t: t.reshape(batch, seq, width)
    vmem = 4 * 2 * seq * A_HEAD_DIM * 4 + 2 * seq * A_HEAD_DIM * 2 + 2 * n_pat * seq * A_HEAD_DIM * 4 \
        + 2 * n_pat * QBLK * 2 * QBLK * 4 + (8 << 20)
    out = pl.pallas_call(
        functools.partial(_attn_prompt_body, seq=seq),
        grid=(batch, heads),
        in_specs=[blk, blk, blk, blk,
                  pl.BlockSpec((n_pat, None, QBLK, 2 * QBLK), lambda b, h: (0, h, 0, 0))],
        out_specs=blk,
        out_shape=jax.ShapeDtypeStruct((batch, seq, width), BF16),
        scratch_shapes=[pltpu.VMEM((n_pat, seq, A_HEAD_DIM), F32), pltpu.VMEM((n_pat, seq, A_HEAD_DIM), F32)],
        compiler_params=_params(("parallel", "parallel"), vmem),
        name="attn_prompt",
    )(r3(q), r3(k), r3(v), r3(gate), bias)
    return out.reshape(batch * seq, width)


def _attn_sample_body(q_ref, kn_ref, vn_ref, gate_ref, bc_ref, bo_ref, *rest):
    n_pat = len(DILATION_PATTERNS)
    kc_refs, vc_refs, o_ref = rest[:n_pat], rest[n_pat:2 * n_pat], rest[2 * n_pat]
    scale = A_HEAD_DIM ** -0.5
    rnd = lambda t: t.astype(BF16).astype(F32)
    q = rnd(q_ref[...])
    kn = rnd(kn_ref[...])
    vn = rnd(vn_ref[...])
    s_new = jnp.sum(q * kn, axis=-1, keepdims=True) * scale + bo_ref[...]
    outs, lses = [], []
    for p in range(n_pat):
        kc = rnd(kc_refs[p][...])
        vc = rnd(vc_refs[p][...])
        s = jnp.sum(q[None] * kc, axis=-1, keepdims=True) * scale + bc_ref[p]
        m = jnp.maximum(jnp.max(s, axis=0), s_new)
        e = jnp.exp(s - m[None])
        e_new = jnp.exp(s_new - m)
        l = jnp.sum(e, axis=0) + e_new
        outs.append((jnp.sum(rnd(e) * vc, axis=0) + rnd(e_new) * vn) / l)
        lses.append(m + jnp.log(l))
    m = functools.reduce(jnp.maximum, lses)
    e = [jnp.exp(t - m) for t in lses]
    den = functools.reduce(jnp.add, e)
    mix = functools.reduce(jnp.add, [(e[p] / den) * outs[p] for p in range(n_pat)])
    o_ref[...] = (mix * gate_ref[...]).astype(o_ref.dtype)


def _attn_sample(q, k_new, v_new, gate, cache_k, cache_v, bias_cache, bias_own):
    nb, past, heads, hd = cache_k.shape
    n_pat = len(DILATION_PATTERNS)
    row = pl.BlockSpec((None, heads, hd), lambda b: (b, 0, 0))
    r3 = lambda t: t.reshape(nb, heads, hd)
    cache_specs, cache_args = [], []
    for cache in (cache_k, cache_v):
        for window, dil in DILATION_PATTERNS:
            last = past // (QBLK * dil) - 1
            cache_specs.append(pl.BlockSpec((None, QBLK, None, heads, hd), lambda b, last=last: (b, last, 0, 0, 0)))
            cache_args.append(cache.reshape(nb, past // dil, dil, heads, hd))
    vmem = 2 * 2 * n_pat * QBLK * heads * hd * 4 + 8 * QBLK * heads * hd * 4 + (8 << 20)
    out = pl.pallas_call(
        _attn_sample_body,
        grid=(nb,),
        in_specs=[row, row, row, row,
                  pl.BlockSpec((n_pat, QBLK, heads, 1), lambda b: (0, 0, 0, 0)),
                  pl.BlockSpec((heads, 1), lambda b: (0, 0))] + cache_specs,
        out_specs=row,
        out_shape=jax.ShapeDtypeStruct((nb, heads, hd), BF16),
        compiler_params=_params(("parallel",), vmem),
        name="attn_sample",
    )(r3(q), r3(k_new), r3(v_new), r3(gate), bias_cache[..., None], bias_own.reshape(heads, 1), *cache_args)
    return out.reshape(nb, heads * hd)


def _gmlp_sample_body(vn_ref, ub_ref, gb_ref, w0_ref, b0_ref, o_ref):
    s = w0_ref[...].astype(BF16).astype(F32) * vn_ref[...].astype(BF16).astype(F32) + b0_ref[...]
    o_ref[...] = (ub_ref[...] * s * gb_ref[...]).astype(o_ref.dtype)


def _gmlp_sample(vn, ub, gb, w_s, b_s):
    m, width = vn.shape
    groups = w_s.shape[0]
    w0 = jnp.repeat(w_s[:, 0, 0], width // groups).reshape(1, width)
    b0 = jnp.repeat(b_s[:, 0], width // groups).reshape(1, width)
    return pl.pallas_call(
        _gmlp_sample_body,
        out_shape=jax.ShapeDtypeStruct((m, width), BF16),
        name="gmlp_sample",
    )(vn, ub, gb, w0, b0)


def _out_ple_body(*refs, n_a, final_norm):
    a_refs = refs[:n_a]
    x_ref, p_ref, wo_ref, wg_ref, wp_ref = refs[n_a:n_a + 5]
    rest = refs[n_a + 5:]
    h = x_ref[...]
    off = 0
    for a_ref in a_refs:
        ka = a_ref.shape[1]
        h = h + _dot(a_ref[...], wo_ref[off:off + ka, :])
        off += ka
    gate = _sigmoid(_dot(h.astype(BF16), wg_ref[...]))
    h = h + gate * _dot(p_ref[...].astype(BF16), wp_ref[...])
    if final_norm:
        fg_ref, o_ref = rest
        o_ref[...] = _rms_norm(h, fg_ref[...])
    else:
        (o_ref,) = rest
        o_ref[...] = h


def _out_ple(a_list, x, p_layers, layer, w_out_bf16, w_gate_layers, w_proj_layers, final_g, tm):
    m, d = x.shape
    pd = p_layers.shape[2]
    row = lambda i: (i, 0)
    fixed = lambda i: (0, 0)
    of_layer = lambda i: (layer, 0, 0)
    in_specs = [pl.BlockSpec((tm, a.shape[1]), row) for a in a_list]
    in_specs += [pl.BlockSpec((tm, d), row), pl.BlockSpec((None, tm, pd), lambda i: (layer, i, 0)),
                 _resident(w_out_bf16.shape, fixed), _resident((None, d, d), of_layer), _resident((None, pd, d), of_layer)]
    args = list(a_list) + [x, p_layers, w_out_bf16, w_gate_layers, w_proj_layers]
    if final_g is not None:
        in_specs.append(pl.BlockSpec((1, d), fixed))
        args.append(final_g.reshape(1, d))
    vmem = (w_out_bf16.size + d * d + pd * d) * 2 + 2 * tm * (2 * d * 4 + pd * 4 + w_out_bf16.shape[0] * 2) \
        + 4 * tm * d * 4 + (6 << 20)
    return pl.pallas_call(
        functools.partial(_out_ple_body, n_a=len(a_list), final_norm=final_g is not None),
        grid=(m // tm,),
        in_specs=in_specs,
        out_specs=pl.BlockSpec((tm, d), row),
        out_shape=jax.ShapeDtypeStruct((m, d), F32),
        compiler_params=_params(("parallel",), vmem),
        name="out_ple",
    )(*args)


def _shifted_inputs(h_ref, prev_ref, g_ref, *, seq_rows, prev_given):
    xn = _rms_norm(h_ref[...], g_ref[...])
    if prev_given:
        return xn, prev_ref[...]
    tm = xn.shape[0]
    before = _rms_norm(prev_ref[...], g_ref[...])[SUBLANES - 1:SUBLANES, :]
    starts_sequence = (pl.program_id(0) * tm) % seq_rows == 0
    before = jnp.where(starts_sequence, 0.0, before)
    first = lax.broadcasted_iota(jnp.int32, xn.shape, 0) == 0
    return xn, jnp.where(first, before, pltpu.roll(xn, 1, 0))


def _rwkv_proj_body(h_ref, prev_ref, g_ref, mu_ref, *rest, seq_rows, prev_given, mixes, acts, lora_mixes):
    xn, x_prev = _shifted_inputs(h_ref, prev_ref, g_ref, seq_rows=seq_rows, prev_given=prev_given)
    xx = x_prev - xn
    mixed = lambda idx: (xn + xx * mu_ref[idx:idx + 1, :]).astype(BF16)
    w_refs, rest = rest[:len(mixes)], rest[len(mixes):]
    if lora_mixes is not None:
        w1_ref, w2_ref, w0_ref, a1_ref, a2_ref, a0_ref = rest[:6]
        rest = rest[6:]
    for i, (mix, act) in enumerate(zip(mixes, acts)):
        z = _dot(mixed(mix), w_refs[i][...])
        rest[i][...] = act(z) if act is not None else z
    if lora_mixes is not None:
        lw_ref, a_ref = rest[len(mixes):]
        zw = w0_ref[...] + _dot(jnp.tanh(_dot(mixed(lora_mixes[0]), w1_ref[...])), w2_ref[...])
        lw_ref[...] = -math.exp(-0.5) * _sigmoid(zw)
        za = a0_ref[...] + _dot(_dot(mixed(lora_mixes[1]), a1_ref[...]), a2_ref[...])
        a_ref[...] = _sigmoid(za)


def _pad_lora(w_down, w_up):
    rank = w_down.shape[1]
    pad = (-rank) % LANES
    return (jnp.pad(w_down, ((0, 0), (0, pad))).astype(BF16), jnp.pad(w_up, ((0, pad), (0, 0))).astype(BF16))


def _rwkv_proj(h, prev, norm_g, mu, weights, mixes, acts, lora, lora_mixes, tm, seq_rows):
    m, d = h.shape
    prev_given = prev is not None
    row = lambda i: (i, 0)
    fixed = lambda i: (0, 0)
    if prev_given:
        specs, args = [pl.BlockSpec((tm, d), row), pl.BlockSpec((tm, d), row)], [h, prev]
    else:
        per = tm // SUBLANES
        specs = [pl.BlockSpec((tm, d), row), pl.BlockSpec((SUBLANES, d), lambda i: (jnp.maximum(i * per - 1, 0), 0))]
        args = [h, h]
    specs += [pl.BlockSpec((1, d), fixed), pl.BlockSpec(mu.shape, fixed)] + [_resident((d, d), fixed)] * len(weights)
    args += [norm_g.reshape(1, d), mu] + [w.astype(BF16) for w in weights]
    n_out = len(weights)
    lora_bytes = 0
    if lora is not None:
        w1, w2, w0, a1, a2, a0 = lora
        w1p, w2p = _pad_lora(w1, w2)
        a1p, a2p = _pad_lora(a1, a2)
        rank = w1p.shape[1]
        vec, down, up = pl.BlockSpec((1, d), fixed), pl.BlockSpec((d, rank), fixed), pl.BlockSpec((rank, d), fixed)
        specs += [down, up, vec, down, up, vec]
        args += [w1p, w2p, w0.reshape(1, d), a1p, a2p, a0.reshape(1, d)]
        n_out += 2
        lora_bytes = 2 * 4 * d * rank * 2
    out = jax.ShapeDtypeStruct((m, d), F32)
    vmem = len(weights) * d * d * 2 + lora_bytes + (2 + 2 * n_out + 8) * tm * d * 4 + (2 << 20)
    return pl.pallas_call(
        functools.partial(_rwkv_proj_body, seq_rows=seq_rows, prev_given=prev_given, mixes=mixes, acts=acts,
                          lora_mixes=lora_mixes if lora is not None else None),
        grid=(m // tm,),
        in_specs=specs,
        out_specs=[pl.BlockSpec((tm, d), row)] * n_out,
        out_shape=[out] * n_out,
        compiler_params=_params(("parallel",), vmem),
        name="rwkv_proj",
    )(*args)


def _rwkv_prompt_body(r_ref, k_ref, v_ref, lw_ref, a_ref, g_ref, kk_ref, ka_ref, rk_ref, gng_ref, gnb_ref,
                      o_ref, st_ref, q2_scr, y0_scr, m_scr, h0_scr, bv_scr, gate_scr, hs_scr, *, seq):
    c_len = RWKV_CHUNK
    n_chunks = seq // c_len
    hd = C_HEAD_DIM
    assert c_len == hd and LANES % hd == 0, "time x time and key x value blocks share one lane tiling"
    lane = lax.broadcasted_iota(jnp.int32, (1, LANES), 1)
    head_masks = (lane < hd, lane >= hd)
    ti = lax.broadcasted_iota(jnp.int32, (c_len, c_len), 0)
    si = lax.broadcasted_iota(jnp.int32, (c_len, c_len), 1)
    tri = (ti >= si).astype(F32)
    tp = lax.broadcasted_iota(jnp.int32, (c_len, LANES), 0)
    sp = lax.broadcasted_iota(jnp.int32, (c_len, LANES), 1) % hd
    incl = tp >= sp
    strict = tp > sp
    eye_c = (tp == sp).astype(F32)
    bi = lax.broadcasted_iota(jnp.int32, (LANES, LANES), 0)
    bj = lax.broadcasted_iota(jnp.int32, (LANES, LANES), 1)
    same_head = (bi // hd) == (bj // hd)
    head_ones = same_head.astype(BF16)

    def head_sum(x):
        return _dot(x, head_ones)

    def stacked(z):
        masks = [jnp.concatenate([mk] * (z.shape[1] // LANES), axis=1) for mk in head_masks]
        return jnp.concatenate([jnp.where(mk, z, 0.0) for mk in masks], axis=0)

    def block_diag(x):
        return jnp.where(same_head, jnp.concatenate([x] * len(head_masks), axis=0), 0.0)

    def diag_blocks(x):
        x = jnp.where(same_head, x, 0.0)
        return x[:hd] + x[hd:]

    step = pl.program_id(0)
    cur = lax.rem(step, 2)
    prev = 1 - cur

    @pl.when(step == 0)
    def _():
        for scr in (q2_scr, y0_scr, m_scr, h0_scr, bv_scr, gate_scr):
            scr[1] = jnp.zeros(scr.shape[1:], scr.dtype)

    chunks = list(range(n_chunks))
    rows = [slice(c * c_len, (c + 1) * c_len) for c in chunks]
    state = [jnp.zeros((hd, LANES), F32)]
    carried = [0]

    def tick():
        c = carried[0]
        if c < n_chunks:
            hs_scr[c] = state[0]
            state[0] = _dot(m_scr[prev, c], stacked(state[0]), (2, 2)) + h0_scr[prev, c]
            carried[0] = c + 1

    def stage(fn, *lists):
        out = []
        for i, args in enumerate(zip(*lists)):
            out.append(fn(*args))
            if i % TICK_EVERY == TICK_EVERY - 1:
                tick()
        return out

    r = [r_ref[rw, :] for rw in rows]
    k = [k_ref[rw, :] for rw in rows]
    v = [v_ref[rw, :] for rw in rows]
    a = [a_ref[rw, :] for rw in rows]
    lw = [lw_ref[rw, :] for rw in rows]
    k2 = [x * (1.0 + (y - 1.0) * ka_ref[...]) for x, y in zip(k, a)]
    kk = [x * kk_ref[...] for x in k]
    norm = stage(lambda x: head_sum(x * x), kk)
    cum = stage(lambda x: _dot(tri, x, (1, 2)), lw)
    bonus = stage(lambda x, y: head_sum(x * y * rk_ref[...]), r, k2)
    kk = [x * lax.rsqrt(jnp.maximum(n, 1e-24)) for x, n in zip(kk, norm)]
    last = [x[c_len - 1:c_len, :] for x in cum]
    b = [x * y for x, y in zip(kk, a)]
    ekk = [x * jnp.exp(cm - w) for x, cm, w in zip(kk, cum, lw)]
    er = [x * jnp.exp(cm) for x, cm in zip(r, cum)]
    inv = [jnp.exp(-cm) for cm in cum]
    eb = [x * y for x, y in zip(b, inv)]
    ek = [x * y for x, y in zip(k2, inv)]
    to_end = [jnp.exp(ls) * x for ls, x in zip(last, inv)]
    eb_end = [x * y for x, y in zip(b, to_end)]
    ek_end = [x * y for x, y in zip(k2, to_end)]
    lhs = [jnp.concatenate([x, y], axis=0) for x, y in zip(ekk, er)]
    g = stage(lambda x, y, z: _dot_nt(x, jnp.concatenate([stacked(y), stacked(z)], axis=0)), lhs, eb, ek)
    l_b = [jnp.where(strict, x[:c_len, :LANES], 0.0) for x in g]
    a_b = [jnp.where(incl, x[c_len:, :LANES], 0.0) for x in g]
    l_k = [jnp.where(strict, x[:c_len, LANES:], 0.0) for x in g]
    a_k = [jnp.where(incl, x[c_len:, LANES:], 0.0) for x in g]
    lakv = stage(lambda x, y, z: _dot(jnp.concatenate([x, y], axis=0), stacked(z)), l_k, a_k, v)
    lkv = [x[:c_len] for x in lakv]
    akv = [x[c_len:] for x in lakv]
    levels = int(math.log2(c_len))
    power = [-x for x in l_b]
    inv_t = [eye_c + x for x in power]
    power = stage(lambda x: _dot(x, block_diag(x)), power)
    for level in range(1, levels):
        if level < levels - 1:
            both = stage(lambda x, y: _dot(jnp.concatenate([x, y], axis=0), block_diag(x)), power, inv_t)
            power = [x[:c_len] for x in both]
            inv_t = [x + y[c_len:] for x, y in zip(inv_t, both)]
        else:
            inv_t = stage(lambda x, y: y + _dot(y, block_diag(x)), power, inv_t)
    tz = stage(lambda x, y, z: _dot(x, stacked(jnp.concatenate([y, z], axis=1))), inv_t, ekk, lkv)
    qu = [jnp.concatenate([x[:, :LANES], -x[:, LANES:]], axis=1) for x in tz]
    ab = stage(lambda x, y: _dot(x, stacked(y)), a_b, qu)
    ends = [jnp.concatenate([x, y], axis=0) for x, y in zip(eb_end, ek_end)]
    tails = [jnp.concatenate([x, jnp.concatenate([jnp.zeros_like(y), y], axis=1)], axis=0) for x, y in zip(qu, v)]
    bq = stage(_dot_tn, ends, tails)
    for c in chunks:
        q2_scr[cur, c] = er[c] - ab[c][:, :LANES]
        y0_scr[cur, c] = ab[c][:, LANES:] + akv[c]
        m_scr[cur, c] = eye_c * jnp.exp(last[c]) - diag_blocks(bq[c][:, :LANES])
        h0_scr[cur, c] = diag_blocks(bq[c][:, LANES:])
        bv_scr[cur, rows[c], :] = bonus[c] * v[c]
        gate_scr[cur, rows[c], :] = g_ref[rows[c], :]
    while carried[0] < n_chunks:
        tick()
    st_ref[...] = _dot_tn(state[0], (ti == si).astype(F32), (3, 1))

    y = [_dot(q2_scr[prev, c], stacked(hs_scr[c])) + y0_scr[prev, c] for c in chunks]
    mean = [head_sum(x) * (1.0 / hd) for x in y]
    dev = [x - mu for x, mu in zip(y, mean)]
    var = [head_sum(x * x) * (1.0 / hd) for x in dev]
    for rw, dv, vr in zip(rows, dev, var):
        yn = dv * lax.rsqrt(vr + GN_EPS) * gng_ref[...] + gnb_ref[...]
        o_ref[rw, :] = ((yn + bv_scr[prev, rw, :]) * gate_scr[prev, rw, :]).astype(o_ref.dtype)


def _rwkv_prompt(r, k, v, lw, a, g, k_k, k_a, r_k, gn_g, gn_b, batch, seq):
    d = r.shape[1]
    pairs = d // LANES
    n_seq = batch * pairs
    n_chunks = seq // RWKV_CHUNK
    hd = C_HEAD_DIM
    this = lambda s: jnp.minimum(s, n_seq - 1)
    last = lambda s: jnp.maximum(s - 1, 0)
    blk = pl.BlockSpec((None, seq, LANES), lambda s: (this(s) // pairs, 0, this(s) % pairs))
    vec = pl.BlockSpec((1, LANES), lambda s: (0, this(s) % pairs))
    vec_last = pl.BlockSpec((1, LANES), lambda s: (0, last(s) % pairs))
    r3 = lambda t: t.reshape(batch, seq, d)
    v2 = lambda t: t.reshape(1, d)
    chunk_maps = pltpu.VMEM((2, n_chunks, RWKV_CHUNK, LANES), F32)
    rows_saved = pltpu.VMEM((2, seq, LANES), F32)
    vmem = 2 * 6 * seq * LANES * 4 + 2 * seq * LANES * 2 + (4 * 2 + 2 * 2 + 1) * seq * LANES * 4 + (24 << 20)
    out, state = pl.pallas_call(
        functools.partial(_rwkv_prompt_body, seq=seq),
        grid=(n_seq + 1,),
        in_specs=[blk] * 6 + [vec] * 3 + [vec_last] * 2,
        out_specs=[pl.BlockSpec((None, seq, LANES), lambda s: (last(s) // pairs, 0, last(s) % pairs)),
                   pl.BlockSpec((None, None, LANES, hd), lambda s: (last(s) // pairs, last(s) % pairs, 0, 0))],
        out_shape=[jax.ShapeDtypeStruct((batch, seq, d), BF16),
                   jax.ShapeDtypeStruct((batch, pairs, LANES, hd), F32)],
        scratch_shapes=[chunk_maps] * 4 + [rows_saved] * 2 + [pltpu.VMEM((n_chunks, RWKV_CHUNK, LANES), F32)],
        compiler_params=_params(("arbitrary",), vmem),
        name="rwkv_prompt",
    )(r3(r), r3(k), r3(v), r3(lw), r3(a), r3(g), v2(k_k), v2(k_a), v2(r_k), v2(gn_g), v2(gn_b))
    return out.reshape(batch * seq, d), state.reshape(batch, d // hd, hd, hd)


def _rwkv_sample_body(s_ref, r_ref, k_ref, lw_ref, a_ref, v_ref, kk_ref, ka_ref, rk_ref, gng_ref, gnb_ref,
                      y_ref, so_ref):
    r, k, a, v = r_ref[...], k_ref[...], a_ref[...], v_ref[...]
    s = s_ref[...]

    def flipped(x):
        return jnp.swapaxes(jnp.broadcast_to(x, s.shape), 1, 2)

    kk = k * kk_ref[...]
    kk = kk / jnp.maximum(jnp.sqrt(jnp.sum(kk * kk, axis=-1, keepdims=True)), 1e-12)
    k2 = k * (1.0 + (a - 1.0) * ka_ref[...])
    sa = -jnp.sum(s * kk, axis=-1, keepdims=True)
    s = s * jnp.exp(lw_ref[...]) + sa * (kk * a) + flipped(v) * k2
    so_ref[...] = s
    y = flipped(jnp.sum(s * r, axis=-1, keepdims=True))[:, 0:1, :]
    mean = jnp.mean(y, axis=-1, keepdims=True)
    var = jnp.mean(jnp.square(y - mean), axis=-1, keepdims=True)
    y = (y - mean) * lax.rsqrt(var + GN_EPS) * gng_ref[...] + gnb_ref[...]
    bonus = jnp.sum(r * k2 * rk_ref[...], axis=-1, keepdims=True)
    y_ref[...] = y + bonus * v


def _rwkv_sample(state, r, k, v, lw, a, k_k, k_a, r_k, gn_g, gn_b):
    nb, heads, hd, _ = state.shape
    rowv = lambda t: t.reshape(nb, heads, 1, hd)
    rowp = lambda t: t.reshape(heads, 1, hd)
    st = pl.BlockSpec((None, heads, hd, hd), lambda b: (b, 0, 0, 0))
    rv = pl.BlockSpec((None, heads, 1, hd), lambda b: (b, 0, 0, 0))
    rp = pl.BlockSpec((heads, 1, hd), lambda b: (0, 0, 0))
    y, new_state = pl.pallas_call(
        _rwkv_sample_body,
        grid=(nb,),
        in_specs=[st, rv, rv, rv, rv, rv, rp, rp, rp, rp, rp],
        out_specs=[rv, st],
        out_shape=[jax.ShapeDtypeStruct((nb, heads, 1, hd), F32), jax.ShapeDtypeStruct(state.shape, F32)],
        compiler_params=_params(("parallel",), 32 << 20),
        name="rwkv_sample",
    )(state, rowv(r), rowv(k), rowv(lw), rowv(a), rowv(v),
      rowp(k_k), rowp(k_a), rowp(r_k), rowp(gn_g), rowp(gn_b))
    return y.reshape(nb, heads * hd), new_state


def _gate_cast_body(y_ref, g_ref, o_ref):
    o_ref[...] = (y_ref[...] * g_ref[...]).astype(o_ref.dtype)


def _gate_cast(y, g):
    return pl.pallas_call(_gate_cast_body, out_shape=jax.ShapeDtypeStruct(y.shape, BF16), name="gate_cast")(y, g)


def kernel(x_prompt, x_sample, cache_a_k, cache_a_v, state_c_wkv, state_c_shift, p_prompt, p_sample, norm_g, final_norm_g, rel_bias, ab_w_in, ab_w_out, b_w_s, b_b_s, b_ln_g, b_ln_b, c_mu, c_w_r, c_w_k, c_w_v, c_w_g, c_w_o, c_w0, c_w1, c_w2, c_a0, c_a1, c_a2, c_k_k, c_k_a, c_r_k, c_gn_g, c_gn_b, ple_w_proj, ple_w_gate):
    batch, seq, d = x_prompt.shape
    nb = x_sample.shape[0]
    assert x_sample.shape[1] == 1 and seq % (QBLK * max(dil for _, dil in DILATION_PATTERNS)) == 0
    assert norm_g.shape[0] == 2, "layer pattern implemented for depth 2: one attention+gMLP layer, one RWKV-7 layer"
    a_heads = cache_a_k.shape[3]
    c_heads = state_c_wkv.shape[2]
    m = batch * seq
    hp = x_prompt.reshape(m, d)
    hs = x_sample.reshape(nb, d)
    pp = p_prompt.reshape(p_prompt.shape[0], m, -1)
    ps = p_sample.reshape(p_sample.shape[0], nb, -1)
    w_gate = ple_w_gate.astype(BF16)
    w_proj = ple_w_proj.astype(BF16)

    w_in = ab_w_in[0].astype(BF16)
    w_out = ab_w_out[0].astype(BF16)
    bias_prompt = _prompt_bias(rel_bias)
    bias_cache, bias_own = _sample_bias(rel_bias)

    q, k, v, ga, ob = _ab_in_proj(hp, norm_g[0], w_in, b_ln_g[0], b_ln_b[0], tm=256, w_s=b_w_s[0], b_s=b_b_s[0])
    oa = _attn_prompt(q, k, v, ga, bias_prompt, batch, seq)
    hp = _out_ple([oa, ob], hp, pp, 0, w_out, w_gate, w_proj, None, tm=256)
    a_k_p = k.reshape(1, batch, seq, a_heads, A_HEAD_DIM)
    a_v_p = v.reshape(1, batch, seq, a_heads, A_HEAD_DIM)

    qs, ks, vs, gas, ubs, vns, gbs = _ab_in_proj(hs, norm_g[0], w_in, b_ln_g[0], b_ln_b[0], tm=nb)
    oas = _attn_sample(qs, ks, vs, gas, cache_a_k[0], cache_a_v[0], bias_cache, bias_own)
    obs = _gmlp_sample(vns, ubs, gbs, b_w_s[0], b_b_s[0])
    hs = _out_ple([oas, obs], hs, ps, 0, w_out, w_gate, w_proj, None, tm=nb)
    a_k_s = ks.reshape(1, nb, 1, a_heads, A_HEAD_DIM)
    a_v_s = vs.reshape(1, nb, 1, a_heads, A_HEAD_DIM)
    b_v_s = vns.reshape(1, nb, 1, -1)

    mu = c_mu[0]
    w_o = c_w_o[0].astype(BF16)
    rk_flat = c_r_k[0].reshape(-1)
    lora = (c_w1[0], c_w2[0], c_w0[0], c_a1[0], c_a2[0], c_a0[0])

    def projections(h, prev, tm, seq_rows):
        r, kx, vx = _rwkv_proj(h, prev, norm_g[1], mu, [c_w_r[0], c_w_k[0], c_w_v[0]], (0, 2, 3), (None, None, None),
                               None, None, tm, seq_rows)
        g, lw, a = _rwkv_proj(h, prev, norm_g[1], mu, [c_w_g[0]], (5,), (_silu,), lora, (1, 4), tm, seq_rows)
        return r, kx, vx, g, lw, a

    r, kx, vx, g, lw, a = projections(hp, None, 256, seq)
    yg, s_p = _rwkv_prompt(r, kx, vx, lw, a, g, c_k_k[0], c_k_a[0], rk_flat, c_gn_g[0], c_gn_b[0], batch, seq)
    y_prompt = _out_ple([yg], hp, pp, 1, w_o, w_gate, w_proj, final_norm_g, tm=256)
    last_rows = hp.reshape(batch, seq, d)[:, seq - SUBLANES:, :].reshape(batch * SUBLANES, d)
    sh_p = _norm_rows(last_rows, norm_g[1]).reshape(batch, SUBLANES, d)[:, SUBLANES - 1]

    rs, kxs, vxs, gs, lws, a_s = projections(hs, state_c_shift[0], nb, 1)
    ys, s_s = _rwkv_sample(state_c_wkv[0], rs, kxs, vxs, lws, a_s, c_k_k[0], c_k_a[0], rk_flat, c_gn_g[0], c_gn_b[0])
    y_sample = _out_ple([_gate_cast(ys, gs)], hs, ps, 1, w_o, w_gate, w_proj, final_norm_g, tm=nb)
    sh_s = _norm_rows(hs, norm_g[1])

    return (y_prompt.reshape(batch, seq, d), y_sample.reshape(nb, 1, d), a_k_p, a_v_p, a_k_s, a_v_s, b_v_s,
            s_p[None], sh_p[None], s_s[None], sh_s[None])


def _norm_rows_body(x_ref, g_ref, o_ref):
    o_ref[...] = _rms_norm(x_ref[...], g_ref[...])


def _norm_rows(x, g):
    return pl.pallas_call(_norm_rows_body, out_shape=jax.ShapeDtypeStruct(x.shape, F32), name="norm_rows")(
        x, g.reshape(1, -1))
```

```python
import functools
import math

import jax
import jax.numpy as jnp
import numpy as np
from jax import lax
from jax.experimental import pallas as pl
from jax.experimental.pallas import tpu as pltpu

F32 = jnp.float32
BF16 = jnp.bfloat16

LANES = 128
SUBLANES = 8
VMEM_BUDGET_BYTES = 56 * 1024 * 1024

A_HEAD_DIM = 128
DILATION_PATTERNS = ((128, 1), (512, 4), (2048, 16))
QBLK = 128
ATTN_GROUP = 16
REL_BUCKETS = 32
REL_MAX_DIST = 2048
CHUNK = 128
C_HEAD_DIM = 64
TICK_EVERY = 12
RWKV_CHUNK = 64
RMS_EPS = 1e-6
LN_EPS = 1e-5
GN_EPS = 64e-5
NEG_INF = -1e30


def _params(semantics, vmem_bytes):
    return pltpu.CompilerParams(dimension_semantics=semantics, vmem_limit_bytes=int(vmem_bytes))


def _resident(shape, index_map):
    return pl.BlockSpec(shape, index_map, pipeline_mode=pl.Buffered(1))


def _bf16_terms(x, n):
    if x.dtype == BF16 or n == 1:
        return [x.astype(BF16)]
    terms, rest = [], x
    for _ in range(n):
        terms.append(rest.astype(BF16))
        rest = rest - terms[-1].astype(F32)
    return terms


def _dot_dims(a, b, dims, terms):
    a_terms = _bf16_terms(a, terms[0])
    b_terms = _bf16_terms(b, terms[1])
    out = None
    for i, at in enumerate(a_terms):
        for j, bt in enumerate(b_terms):
            if i + j < max(len(a_terms), len(b_terms)):
                part = lax.dot_general(at, bt, (dims, ((), ())), preferred_element_type=F32)
                out = part if out is None else out + part
    return out


def _dot(a, b, terms=(1, 1)):
    return _dot_dims(a, b, ((1,), (0,)), terms)


def _dot_nt(a, b, terms=(1, 1)):
    return _dot_dims(a, b, ((1,), (1,)), terms)


def _dot_tn(a, b, terms=(1, 1)):
    return _dot_dims(a, b, ((0,), (0,)), terms)


def _rms_norm(x, g):
    return x * lax.rsqrt(jnp.mean(x * x, axis=-1, keepdims=True) + RMS_EPS) * g


def _layer_norm(x, g, b):
    mu = jnp.mean(x, axis=-1, keepdims=True)
    var = jnp.mean(jnp.square(x - mu), axis=-1, keepdims=True)
    return (x - mu) * lax.rsqrt(var + LN_EPS) * g + b


def _sigmoid(x):
    return 0.5 * jnp.tanh(0.5 * x) + 0.5


def _silu(x):
    return x * _sigmoid(x)


def _ab_in_body(x_ref, g_ref, w_ref, lng_ref, lnb_ref, *rest, mix_chunks):
    xn = _rms_norm(x_ref[...], g_ref[...]).astype(BF16)
    if mix_chunks:
        ws_ref, bs_ref, q_ref, k_ref, v_ref, ga_ref, ob_ref = rest
    else:
        q_ref, k_ref, v_ref, ga_ref, ub_ref, vn_ref, gb_ref = rest
    width = q_ref.shape[1]
    column_group = lambda idx: _dot(xn, w_ref[:, idx * width:(idx + 1) * width])
    q_ref[...] = column_group(0)
    k_ref[...] = column_group(1)
    v_ref[...] = column_group(2)
    ga_ref[...] = _silu(column_group(3))
    ub = jax.nn.gelu(column_group(4))
    vn = _layer_norm(jax.nn.gelu(column_group(5)), lng_ref[...], lnb_ref[...])
    gb = _silu(column_group(6))
    if not mix_chunks:
        ub_ref[...], vn_ref[...], gb_ref[...] = ub, vn, gb
        return
    ii = lax.broadcasted_iota(jnp.int32, (CHUNK, CHUNK), 0)
    jj = lax.broadcasted_iota(jnp.int32, (CHUNK, CHUNK), 1)
    vn = vn.astype(BF16)
    for grp in range(ws_ref.shape[0]):
        w = jnp.where(jj <= ii, ws_ref[grp], 0.0).astype(BF16)
        cols = slice(grp * LANES, (grp + 1) * LANES)
        for c in range(xn.shape[0] // CHUNK):
            rows = slice(c * CHUNK, (c + 1) * CHUNK)
            s = _dot(w, vn[rows, cols]) + bs_ref[:, cols]
            ob_ref[rows, cols] = (ub[rows, cols] * s * gb[rows, cols]).astype(ob_ref.dtype)


def _ab_in_proj(x, norm_g, w_in_bf16, ln_g, ln_b, tm, w_s=None, b_s=None):
    m, d = x.shape
    width = w_in_bf16.shape[1] // 7
    out = jax.ShapeDtypeStruct((m, width), F32)
    row = lambda i: (i, 0)
    fixed = lambda i: (0, 0)
    mix_chunks = w_s is not None
    in_specs = [pl.BlockSpec((tm, d), row), pl.BlockSpec((1, d), fixed), _resident(w_in_bf16.shape, fixed),
                pl.BlockSpec((1, width), fixed), pl.BlockSpec((1, width), fixed)]
    args = [x, norm_g.reshape(1, d), w_in_bf16, ln_g.reshape(1, width), ln_b.reshape(1, width)]
    if mix_chunks:
        assert tm % CHUNK == 0 and width == w_s.shape[0] * LANES
        groups = w_s.shape[0]
        in_specs += [pl.BlockSpec((groups, CHUNK, CHUNK), lambda i: (0, 0, 0)), pl.BlockSpec((CHUNK, width), fixed)]
        args += [w_s, jnp.repeat(b_s.T, width // groups, axis=1)]
        out_shape = [out] * 4 + [jax.ShapeDtypeStruct((m, width), BF16)]
    else:
        out_shape = [out] * 7
    vmem = w_in_bf16.size * 2 + 2 * tm * d * 4 + len(out_shape) * 2 * tm * width * 4 + tm * d * 2 + 10 * tm * width * 4
    return pl.pallas_call(
        functools.partial(_ab_in_body, mix_chunks=mix_chunks),
        grid=(m // tm,),
        in_specs=in_specs,
        out_specs=[pl.BlockSpec((tm, width), row)] * len(out_shape),
        out_shape=out_shape,
        compiler_params=_params(("parallel",), vmem),
        name="ab_in_proj",
    )(*args)


def _t5_bucket(dist):
    n_exact = REL_BUCKETS // 2
    d = np.maximum(dist, 1).astype(np.float32)
    log_b = n_exact + (np.log(d / n_exact) / math.log(REL_MAX_DIST / n_exact) * (REL_BUCKETS - n_exact)).astype(np.int32)
    return np.where(dist < n_exact, dist, np.minimum(log_b, REL_BUCKETS - 1)).astype(np.int32)


def _bias_at(rel_bias, dist):
    one_hot = jnp.asarray(_t5_bucket(dist)[..., None] == np.arange(REL_BUCKETS), F32)
    return jnp.einsum("...k,kh->...h", one_hot, rel_bias.astype(F32), precision=lax.Precision.HIGHEST)


def _prompt_bias(rel_bias):
    i = np.arange(QBLK)[:, None]
    j = np.arange(2 * QBLK)[None, :]
    steps = QBLK + i - j
    tables = []
    for window, dil in DILATION_PATTERNS:
        band = (steps >= 0) & (steps <= window // dil)
        bias = jnp.moveaxis(_bias_at(rel_bias, np.clip(steps, 0, None) * dil), -1, 0)
        tables.append(jnp.where(band[None], bias, NEG_INF))
    return jnp.stack(tables)


def _sample_bias(rel_bias):
    back = QBLK - np.arange(QBLK)
    cache = jnp.stack([_bias_at(rel_bias, back * dil) for _, dil in DILATION_PATTERNS])
    own = _bias_at(rel_bias, np.zeros((), np.int32))
    return cache, own


def _attn_prompt_body(q_ref, k_ref, v_ref, gate_ref, bias_ref, o_ref, o_scr, lse_scr, *, seq):
    scale = A_HEAD_DIM ** -0.5

    def blocks(p, dil, starts, with_prev):
        ds = lambda st: pl.ds(st, QBLK, stride=dil) if dil > 1 else pl.ds(st, QBLK)
        rows = [ds(st) for st in starts]
        q = [q_ref[rw, :].astype(BF16) for rw in rows]
        k = [k_ref[rw, :] for rw in rows]
        v = [v_ref[rw, :] for rw in rows]
        if with_prev:
            prev = [ds(st - QBLK * dil) for st in starts]
            k = [jnp.concatenate([k_ref[pv, :], x], axis=0) for pv, x in zip(prev, k)]
            v = [jnp.concatenate([v_ref[pv, :], x], axis=0) for pv, x in zip(prev, v)]
            bias = bias_ref[p]
        else:
            bias = bias_ref[p, :, QBLK:]
        s = [_dot_nt(x, y) * scale + bias for x, y in zip(q, k)]
        m = [jnp.max(x, axis=-1, keepdims=True) for x in s]
        e = [jnp.exp(x - y) for x, y in zip(s, m)]
        l = [jnp.sum(x, axis=-1, keepdims=True) for x in e]
        o = [_dot(x, y) * (1.0 / z) for x, y, z in zip(e, v, l)]
        for rw, x, y, z in zip(rows, o, m, l):
            o_scr[p, rw, :] = x
            lse_scr[p, rw, :] = jnp.broadcast_to(y + jnp.log(z), (QBLK, A_HEAD_DIM))

    def widest_group(count):
        return max(g for g in range(1, ATTN_GROUP + 1) if count % g == 0)

    for p, (_, dil) in enumerate(DILATION_PATTERNS):
        nb = seq // (QBLK * dil)
        shift = dil.bit_length() - 1
        g_first = widest_group(dil)

        def first(i, carry, p=p, dil=dil, g=g_first):
            blocks(p, dil, [i * g + u for u in range(g)], False)
            return carry

        lax.fori_loop(0, dil // g_first, first, 0)
        if nb > 1:
            g_rest = widest_group(dil * (nb - 1))

            def rest(i, carry, p=p, dil=dil, shift=shift, g=g_rest):
                idx = [i * g + u for u in range(g)]
                blocks(p, dil, [((x >> shift) + 1) * (QBLK * dil) + (x & (dil - 1)) for x in idx], True)
                return carry

            lax.fori_loop(0, dil * (nb - 1) // g_rest, rest, 0)

    lse = [lse_scr[p] for p in range(len(DILATION_PATTERNS))]
    m = functools.reduce(jnp.maximum, lse)
    e = [jnp.exp(t - m) for t in lse]
    den = functools.reduce(jnp.add, e)
    inv_den = 1.0 / den
    mix = functools.reduce(jnp.add, [(e[p] * inv_den) * o_scr[p] for p in range(len(e))])
    o_ref[...] = (mix * gate_ref[...]).astype(o_ref.dtype)


def _attn_prompt(q, k, v, gate, bias, batch, seq):
    width = q.shape[1]
    heads = width // A_HEAD_DIM
    n_pat = len(DILATION_PATTERNS)
    blk = pl.BlockSpec((None, seq, A_HEAD_DIM), lambda b, h: (b, 0, h))
    r3 = lambda t: t.reshape(batch, seq, width)
    vmem = 4 * 2 * seq * A_HEAD_DIM * 4 + 2 * seq * A_HEAD_DIM * 2 + 2 * n_pat * seq * A_HEAD_DIM * 4 \
        + 2 * n_pat * QBLK * 2 * QBLK * 4 + (8 << 20)
    out = pl.pallas_call(
        functools.partial(_attn_prompt_body, seq=seq),
        grid=(batch, heads),
        in_specs=[blk, blk, blk, blk,
                  pl.BlockSpec((n_pat, None, QBLK, 2 * QBLK), lambda b, h: (0, h, 0, 0))],
        out_specs=blk,
        out_shape=jax.ShapeDtypeStruct((batch, seq, width), BF16),
        scratch_shapes=[pltpu.VMEM((n_pat, seq, A_HEAD_DIM), F32), pltpu.VMEM((n_pat, seq, A_HEAD_DIM), F32)],
        compiler_params=_params(("parallel", "parallel"), vmem),
        name="attn_prompt",
    )(r3(q), r3(k), r3(v), r3(gate), bias)
    return out.reshape(batch * seq, width)


def _attn_sample_body(q_ref, kn_ref, vn_ref, gate_ref, bc_ref, bo_ref, *rest):
    n_pat = len(DILATION_PATTERNS)
    kc_refs, vc_refs, o_ref = rest[:n_pat], rest[n_pat:2 * n_pat], rest[2 * n_pat]
    scale = A_HEAD_DIM ** -0.5
    rnd = lambda t: t.astype(BF16).astype(F32)
    q = rnd(q_ref[...])
    kn = rnd(kn_ref[...])
    vn = rnd(vn_ref[...])
    s_new = jnp.sum(q * kn, axis=-1, keepdims=True) * scale + bo_ref[...]
    outs, lses = [], []
    for p in range(n_pat):
        kc = rnd(kc_refs[p][...])
        vc = rnd(vc_refs[p][...])
        s = jnp.sum(q[None] * kc, axis=-1, keepdims=True) * scale + bc_ref[p]
        m = jnp.maximum(jnp.max(s, axis=0), s_new)
        e = jnp.exp(s - m[None])
        e_new = jnp.exp(s_new - m)
        l = jnp.sum(e, axis=0) + e_new
        outs.append((jnp.sum(rnd(e) * vc, axis=0) + rnd(e_new) * vn) / l)
        lses.append(m + jnp.log(l))
    m = functools.reduce(jnp.maximum, lses)
    e = [jnp.exp(t - m) for t in lses]
    den = functools.reduce(jnp.add, e)
    mix = functools.reduce(jnp.add, [(e[p] / den) * outs[p] for p in range(n_pat)])
    o_ref[...] = (mix * gate_ref[...]).astype(o_ref.dtype)


def _attn_sample(q, k_new, v_new, gate, cache_k, cache_v, bias_cache, bias_own):
    nb, past, heads, hd = cache_k.shape
    n_pat = len(DILATION_PATTERNS)
    row = pl.BlockSpec((None, heads, hd), lambda b: (b, 0, 0))
    r3 = lambda t: t.reshape(nb, heads, hd)
    cache_specs, cache_args = [], []
    for cache in (cache_k, cache_v):
        for window, dil in DILATION_PATTERNS:
            last = past // (QBLK * dil) - 1
            cache_specs.append(pl.BlockSpec((None, QBLK, None, heads, hd), lambda b, last=last: (b, last, 0, 0, 0)))
            cache_args.append(cache.reshape(nb, past // dil, dil, heads, hd))
    vmem = 2 * 2 * n_pat * QBLK * heads * hd * 4 + 8 * QBLK * heads * hd * 4 + (8 << 20)
    out = pl.pallas_call(
        _attn_sample_body,
        grid=(nb,),
        in_specs=[row, row, row, row,
                  pl.BlockSpec((n_pat, QBLK, heads, 1), lambda b: (0, 0, 0, 0)),
                  pl.BlockSpec((heads, 1), lambda b: (0, 0))] + cache_specs,
        out_specs=row,
        out_shape=jax.ShapeDtypeStruct((nb, heads, hd), BF16),
        compiler_params=_params(("parallel",), vmem),
        name="attn_sample",
    )(r3(q), r3(k_new), r3(v_new), r3(gate), bias_cache[..., None], bias_own.reshape(heads, 1), *cache_args)
    return out.reshape(nb, heads * hd)


def _gmlp_sample_body(vn_ref, ub_ref, gb_ref, w0_ref, b0_ref, o_ref):
    s = w0_ref[...].astype(BF16).astype(F32) * vn_ref[...].astype(BF16).astype(F32) + b0_ref[...]
    o_ref[...] = (ub_ref[...] * s * gb_ref[...]).astype(o_ref.dtype)


def _gmlp_sample(vn, ub, gb, w_s, b_s):
    m, width = vn.shape
    groups = w_s.shape[0]
    w0 = jnp.repeat(w_s[:, 0, 0], width // groups).reshape(1, width)
    b0 = jnp.repeat(b_s[:, 0], width // groups).reshape(1, width)
    return pl.pallas_call(
        _gmlp_sample_body,
        out_shape=jax.ShapeDtypeStruct((m, width), BF16),
        name="gmlp_sample",
    )(vn, ub, gb, w0, b0)


def _out_ple_body(*refs, n_a, final_norm):
    a_refs = refs[:n_a]
    x_ref, p_ref, wo_ref, wg_ref, wp_ref = refs[n_a:n_a + 5]
    rest = refs[n_a + 5:]
    h = x_ref[...]
    off = 0
    for a_ref in a_refs:
        ka = a_ref.shape[1]
        h = h + _dot(a_ref[...], wo_ref[off:off + ka, :])
        off += ka
    gate = _sigmoid(_dot(h.astype(BF16), wg_ref[...]))
    h = h + gate * _dot(p_ref[...].astype(BF16), wp_ref[...])
    if final_norm:
        fg_ref, o_ref = rest
        o_ref[...] = _rms_norm(h, fg_ref[...])
    else:
        (o_ref,) = rest
        o_ref[...] = h


def _out_ple(a_list, x, p_layers, layer, w_out_bf16, w_gate_layers, w_proj_layers, final_g, tm):
    m, d = x.shape
    pd = p_layers.shape[2]
    row = lambda i: (i, 0)
    fixed = lambda i: (0, 0)
    of_layer = lambda i: (layer, 0, 0)
    in_specs = [pl.BlockSpec((tm, a.shape[1]), row) for a in a_list]
    in_specs += [pl.BlockSpec((tm, d), row), pl.BlockSpec((None, tm, pd), lambda i: (layer, i, 0)),
                 _resident(w_out_bf16.shape, fixed), _resident((None, d, d), of_layer), _resident((None, pd, d), of_layer)]
    args = list(a_list) + [x, p_layers, w_out_bf16, w_gate_layers, w_proj_layers]
    if final_g is not None:
        in_specs.append(pl.BlockSpec((1, d), fixed))
        args.append(final_g.reshape(1, d))
    vmem = (w_out_bf16.size + d * d + pd * d) * 2 + 2 * tm * (2 * d * 4 + pd * 4 + w_out_bf16.shape[0] * 2) \
        + 4 * tm * d * 4 + (6 << 20)
    return pl.pallas_call(
        functools.partial(_out_ple_body, n_a=len(a_list), final_norm=final_g is not None),
        grid=(m // tm,),
        in_specs=in_specs,
        out_specs=pl.BlockSpec((tm, d), row),
        out_shape=jax.ShapeDtypeStruct((m, d), F32),
        compiler_params=_params(("parallel",), vmem),
        name="out_ple",
    )(*args)


def _shifted_inputs(h_ref, prev_ref, g_ref, *, seq_rows, prev_given):
    xn = _rms_norm(h_ref[...], g_ref[...])
    if prev_given:
        return xn, prev_ref[...]
    tm = xn.shape[0]
    before = _rms_norm(prev_ref[...], g_ref[...])[SUBLANES - 1:SUBLANES, :]
    starts_sequence = (pl.program_id(0) * tm) % seq_rows == 0
    before = jnp.where(starts_sequence, 0.0, before)
    first = lax.broadcasted_iota(jnp.int32, xn.shape, 0) == 0
    return xn, jnp.where(first, before, pltpu.roll(xn, 1, 0))


def _rwkv_proj_body(h_ref, prev_ref, g_ref, mu_ref, *rest, seq_rows, prev_given, mixes, acts, lora_mixes):
    xn, x_prev = _shifted_inputs(h_ref, prev_ref, g_ref, seq_rows=seq_rows, prev_given=prev_given)
    xx = x_prev - xn
    mixed = lambda idx: (xn + xx * mu_ref[idx:idx + 1, :]).astype(BF16)
    w_refs, rest = rest[:len(mixes)], rest[len(mixes):]
    if lora_mixes is not None:
        w1_ref, w2_ref, w0_ref, a1_ref, a2_ref, a0_ref = rest[:6]
        rest = rest[6:]
    for i, (mix, act) in enumerate(zip(mixes, acts)):
        z = _dot(mixed(mix), w_refs[i][...])
        rest[i][...] = act(z) if act is not None else z
    if lora_mixes is not None:
        lw_ref, a_ref = rest[len(mixes):]
        zw = w0_ref[...] + _dot(jnp.tanh(_dot(mixed(lora_mixes[0]), w1_ref[...])), w2_ref[...])
        lw_ref[...] = -math.exp(-0.5) * _sigmoid(zw)
        za = a0_ref[...] + _dot(_dot(mixed(lora_mixes[1]), a1_ref[...]), a2_ref[...])
        a_ref[...] = _sigmoid(za)


def _pad_lora(w_down, w_up):
    rank = w_down.shape[1]
    pad = (-rank) % LANES
    return (jnp.pad(w_down, ((0, 0), (0, pad))).astype(BF16), jnp.pad(w_up, ((0, pad), (0, 0))).astype(BF16))


def _rwkv_proj(h, prev, norm_g, mu, weights, mixes, acts, lora, lora_mixes, tm, seq_rows):
    m, d = h.shape
    prev_given = prev is not None
    row = lambda i: (i, 0)
    fixed = lambda i: (0, 0)
    if prev_given:
        specs, args = [pl.BlockSpec((tm, d), row), pl.BlockSpec((tm, d), row)], [h, prev]
    else:
        per = tm // SUBLANES
        specs = [pl.BlockSpec((tm, d), row), pl.BlockSpec((SUBLANES, d), lambda i: (jnp.maximum(i * per - 1, 0), 0))]
        args = [h, h]
    specs += [pl.BlockSpec((1, d), fixed), pl.BlockSpec(mu.shape, fixed)] + [_resident((d, d), fixed)] * len(weights)
    args += [norm_g.reshape(1, d), mu] + [w.astype(BF16) for w in weights]
    n_out = len(weights)
    lora_bytes = 0
    if lora is not None:
        w1, w2, w0, a1, a2, a0 = lora
        w1p, w2p = _pad_lora(w1, w2)
        a1p, a2p = _pad_lora(a1, a2)
        rank = w1p.shape[1]
        vec, down, up = pl.BlockSpec((1, d), fixed), pl.BlockSpec((d, rank), fixed), pl.BlockSpec((rank, d), fixed)
        specs += [down, up, vec, down, up, vec]
        args += [w1p, w2p, w0.reshape(1, d), a1p, a2p, a0.reshape(1, d)]
        n_out += 2
        lora_bytes = 2 * 4 * d * rank * 2
    out = jax.ShapeDtypeStruct((m, d), F32)
    vmem = len(weights) * d * d * 2 + lora_bytes + (2 + 2 * n_out + 8) * tm * d * 4 + (2 << 20)
    return pl.pallas_call(
        functools.partial(_rwkv_proj_body, seq_rows=seq_rows, prev_given=prev_given, mixes=mixes, acts=acts,
                          lora_mixes=lora_mixes if lora is not None else None),
        grid=(m // tm,),
        in_specs=specs,
        out_specs=[pl.BlockSpec((tm, d), row)] * n_out,
        out_shape=[out] * n_out,
        compiler_params=_params(("parallel",), vmem),
        name="rwkv_proj",
    )(*args)


def _rwkv_prompt_body(r_ref, k_ref, v_ref, lw_ref, a_ref, g_ref, kk_ref, ka_ref, rk_ref, gng_ref, gnb_ref,
                      o_ref, st_ref, q2_scr, y0_scr, m_scr, h0_scr, bv_scr, gate_scr, hs_scr, *, seq):
    c_len = RWKV_CHUNK
    n_chunks = seq // c_len
    hd = C_HEAD_DIM
    assert c_len == hd and LANES % hd == 0, "time x time and key x value blocks share one lane tiling"
    lane = lax.broadcasted_iota(jnp.int32, (1, LANES), 1)
    head_masks = (lane < hd, lane >= hd)
    ti = lax.broadcasted_iota(jnp.int32, (c_len, c_len), 0)
    si = lax.broadcasted_iota(jnp.int32, (c_len, c_len), 1)
    tri = (ti >= si).astype(F32)
    tp = lax.broadcasted_iota(jnp.int32, (c_len, LANES), 0)
    sp = lax.broadcasted_iota(jnp.int32, (c_len, LANES), 1) % hd
    incl = tp >= sp
    strict = tp > sp
    eye_c = (tp == sp).astype(F32)
    bi = lax.broadcasted_iota(jnp.int32, (LANES, LANES), 0)
    bj = lax.broadcasted_iota(jnp.int32, (LANES, LANES), 1)
    same_head = (bi // hd) == (bj // hd)
    head_ones = same_head.astype(BF16)

    def head_sum(x):
        return _dot(x, head_ones)

    def stacked(z):
        masks = [jnp.concatenate([mk] * (z.shape[1] // LANES), axis=1) for mk in head_masks]
        return jnp.concatenate([jnp.where(mk, z, 0.0) for mk in masks], axis=0)

    def block_diag(x):
        return jnp.where(same_head, jnp.concatenate([x] * len(head_masks), axis=0), 0.0)

    def diag_blocks(x):
        x = jnp.where(same_head, x, 0.0)
        return x[:hd] + x[hd:]

    step = pl.program_id(0)
    cur = lax.rem(step, 2)
    prev = 1 - cur

    @pl.when(step == 0)
    def _():
        for scr in (q2_scr, y0_scr, m_scr, h0_scr, bv_scr, gate_scr):
            scr[1] = jnp.zeros(scr.shape[1:], scr.dtype)

    chunks = list(range(n_chunks))
    rows = [slice(c * c_len, (c + 1) * c_len) for c in chunks]
    state = [jnp.zeros((hd, LANES), F32)]
    carried = [0]

    def tick():
        c = carried[0]
        if c < n_chunks:
            hs_scr[c] = state[0]
            state[0] = _dot(m_scr[prev, c], stacked(state[0]), (2, 2)) + h0_scr[prev, c]
            carried[0] = c + 1

    def stage(fn, *lists):
        out = []
        for i, args in enumerate(zip(*lists)):
            out.append(fn(*args))
            if i % TICK_EVERY == TICK_EVERY - 1:
                tick()
        return out

    r = [r_ref[rw, :] for rw in rows]
    k = [k_ref[rw, :] for rw in rows]
    v = [v_ref[rw, :] for rw in rows]
    a = [a_ref[rw, :] for rw in rows]
    lw = [lw_ref[rw, :] for rw in rows]
    k2 = [x * (1.0 + (y - 1.0) * ka_ref[...]) for x, y in zip(k, a)]
    kk = [x * kk_ref[...] for x in k]
    norm = stage(lambda x: head_sum(x * x), kk)
    cum = stage(lambda x: _dot(tri, x, (1, 2)), lw)
    bonus = stage(lambda x, y: head_sum(x * y * rk_ref[...]), r, k2)
    kk = [x * lax.rsqrt(jnp.maximum(n, 1e-24)) for x, n in zip(kk, norm)]
    last = [x[c_len - 1:c_len, :] for x in cum]
    b = [x * y for x, y in zip(kk, a)]
    ekk = [x * jnp.exp(cm - w) for x, cm, w in zip(kk, cum, lw)]
    tick()
    er = [x * jnp.exp(cm) for x, cm in zip(r, cum)]
    inv = [jnp.exp(-cm) for cm in cum]
    eb = [x * y for x, y in zip(b, inv)]
    ek = [x * y for x, y in zip(k2, inv)]
    tick()
    w_end = [jnp.exp(ls) for ls in last]
    eb_end = [x * y for x, y in zip(eb, w_end)]
    ek_end = [x * y for x, y in zip(ek, w_end)]
    lhs = [jnp.concatenate([x, y], axis=0) for x, y in zip(ekk, er)]
    g = stage(lambda x, y, z: _dot_nt(x, jnp.concatenate([stacked(y), stacked(z)], axis=0)), lhs, eb, ek)
    l_b = [jnp.where(strict, x[:c_len, :LANES], 0.0) for x in g]
    a_b = [jnp.where(incl, x[c_len:, :LANES], 0.0) for x in g]
    l_k = [jnp.where(strict, x[:c_len, LANES:], 0.0) for x in g]
    a_k = [jnp.where(incl, x[c_len:, LANES:], 0.0) for x in g]
    lakv = stage(lambda x, y, z: _dot(jnp.concatenate([x, y], axis=0), stacked(z)), l_k, a_k, v)
    lkv = [x[:c_len] for x in lakv]
    akv = [x[c_len:] for x in lakv]
    levels = int(math.log2(c_len))
    power = [-x for x in l_b]
    inv_t = [eye_c + x for x in power]
    power = stage(lambda x: _dot(x, block_diag(x)), power)
    for level in range(1, levels):
        if level < levels - 1:
            both = stage(lambda x, y: _dot(jnp.concatenate([x, y], axis=0), block_diag(x)), power, inv_t)
            power = [x[:c_len] for x in both]
            inv_t = [x + y[c_len:] for x, y in zip(inv_t, both)]
        else:
            inv_t = stage(lambda x, y: y + _dot(y, block_diag(x)), power, inv_t)
    tz = stage(lambda x, y, z: _dot(x, stacked(jnp.concatenate([y, z], axis=1))), inv_t, ekk, lkv)
    qu = [jnp.concatenate([x[:, :LANES], -x[:, LANES:]], axis=1) for x in tz]
    ab = stage(lambda x, y: _dot(x, stacked(y)), a_b, qu)
    ends = [jnp.concatenate([x, y], axis=0) for x, y in zip(eb_end, ek_end)]
    tails = [jnp.concatenate([x, jnp.concatenate([jnp.zeros_like(y), y], axis=1)], axis=0) for x, y in zip(qu, v)]
    bq = stage(_dot_tn, ends, tails)
    for c in chunks:
        q2_scr[cur, c] = er[c] - ab[c][:, :LANES]
        y0_scr[cur, c] = ab[c][:, LANES:] + akv[c]
        m_scr[cur, c] = eye_c * jnp.exp(last[c]) - diag_blocks(bq[c][:, :LANES])
        h0_scr[cur, c] = diag_blocks(bq[c][:, LANES:])
        bv_scr[cur, rows[c], :] = bonus[c] * v[c]
        gate_scr[cur, rows[c], :] = g_ref[rows[c], :]
    def emit(which):
        y = [_dot(q2_scr[prev, c], stacked(hs_scr[c])) + y0_scr[prev, c] for c in which]
        tick()
        mean = [head_sum(x) * (1.0 / hd) for x in y]
        tick()
        dev = [x - mu for x, mu in zip(y, mean)]
        var = [head_sum(x * x) * (1.0 / hd) for x in dev]
        tick()
        for c, dv, vr in zip(which, dev, var):
            yn = dv * lax.rsqrt(vr + GN_EPS) * gng_ref[...] + gnb_ref[...]
            o_ref[rows[c], :] = ((yn + bv_scr[prev, rows[c], :]) * gate_scr[prev, rows[c], :]).astype(o_ref.dtype)

    ready = carried[0]
    emit(chunks[:ready])
    while carried[0] < n_chunks:
        tick()
    st_ref[...] = _dot_tn(state[0], (ti == si).astype(F32), (3, 1))
    emit(chunks[ready:])


def _rwkv_prompt(r, k, v, lw, a, g, k_k, k_a, r_k, gn_g, gn_b, batch, seq):
    d = r.shape[1]
    pairs = d // LANES
    n_seq = batch * pairs
    n_chunks = seq // RWKV_CHUNK
    hd = C_HEAD_DIM
    this = lambda s: jnp.minimum(s, n_seq - 1)
    last = lambda s: jnp.maximum(s - 1, 0)
    blk = pl.BlockSpec((None, seq, LANES), lambda s: (this(s) // pairs, 0, this(s) % pairs))
    vec = pl.BlockSpec((1, LANES), lambda s: (0, this(s) % pairs))
    vec_last = pl.BlockSpec((1, LANES), lambda s: (0, last(s) % pairs))
    r3 = lambda t: t.reshape(batch, seq, d)
    v2 = lambda t: t.reshape(1, d)
    chunk_maps = pltpu.VMEM((2, n_chunks, RWKV_CHUNK, LANES), F32)
    rows_saved = pltpu.VMEM((2, seq, LANES), F32)
    vmem = 2 * 6 * seq * LANES * 4 + 2 * seq * LANES * 2 + (4 * 2 + 2 * 2 + 1) * seq * LANES * 4 + (24 << 20)
    out, state = pl.pallas_call(
        functools.partial(_rwkv_prompt_body, seq=seq),
        grid=(n_seq + 1,),
        in_specs=[blk] * 6 + [vec] * 3 + [vec_last] * 2,
        out_specs=[pl.BlockSpec((None, seq, LANES), lambda s: (last(s) // pairs, 0, last(s) % pairs)),
                   pl.BlockSpec((None, None, LANES, hd), lambda s: (last(s) // pairs, last(s) % pairs, 0, 0))],
        out_shape=[jax.ShapeDtypeStruct((batch, seq, d), BF16),
                   jax.ShapeDtypeStruct((batch, pairs, LANES, hd), F32)],
        scratch_shapes=[chunk_maps] * 4 + [rows_saved] * 2 + [pltpu.VMEM((n_chunks, RWKV_CHUNK, LANES), F32)],
        compiler_params=_params(("arbitrary",), vmem),
        name="rwkv_prompt",
    )(r3(r), r3(k), r3(v), r3(lw), r3(a), r3(g), v2(k_k), v2(k_a), v2(r_k), v2(gn_g), v2(gn_b))
    return out.reshape(batch * seq, d), state.reshape(batch, d // hd, hd, hd)


def _rwkv_sample_body(s_ref, r_ref, k_ref, lw_ref, a_ref, v_ref, kk_ref, ka_ref, rk_ref, gng_ref, gnb_ref,
                      y_ref, so_ref):
    r, k, a, v = r_ref[...], k_ref[...], a_ref[...], v_ref[...]
    s = s_ref[...]

    def flipped(x):
        return jnp.swapaxes(jnp.broadcast_to(x, s.shape), 1, 2)

    kk = k * kk_ref[...]
    kk = kk / jnp.maximum(jnp.sqrt(jnp.sum(kk * kk, axis=-1, keepdims=True)), 1e-12)
    k2 = k * (1.0 + (a - 1.0) * ka_ref[...])
    sa = -jnp.sum(s * kk, axis=-1, keepdims=True)
    s = s * jnp.exp(lw_ref[...]) + sa * (kk * a) + flipped(v) * k2
    so_ref[...] = s
    y = flipped(jnp.sum(s * r, axis=-1, keepdims=True))[:, 0:1, :]
    mean = jnp.mean(y, axis=-1, keepdims=True)
    var = jnp.mean(jnp.square(y - mean), axis=-1, keepdims=True)
    y = (y - mean) * lax.rsqrt(var + GN_EPS) * gng_ref[...] + gnb_ref[...]
    bonus = jnp.sum(r * k2 * rk_ref[...], axis=-1, keepdims=True)
    y_ref[...] = y + bonus * v


def _rwkv_sample(state, r, k, v, lw, a, k_k, k_a, r_k, gn_g, gn_b):
    nb, heads, hd, _ = state.shape
    rowv = lambda t: t.reshape(nb, heads, 1, hd)
    rowp = lambda t: t.reshape(heads, 1, hd)
    st = pl.BlockSpec((None, heads, hd, hd), lambda b: (b, 0, 0, 0))
    rv = pl.BlockSpec((None, heads, 1, hd), lambda b: (b, 0, 0, 0))
    rp = pl.BlockSpec((heads, 1, hd), lambda b: (0, 0, 0))
    y, new_state = pl.pallas_call(
        _rwkv_sample_body,
        grid=(nb,),
        in_specs=[st, rv, rv, rv, rv, rv, rp, rp, rp, rp, rp],
        out_specs=[rv, st],
        out_shape=[jax.ShapeDtypeStruct((nb, heads, 1, hd), F32), jax.ShapeDtypeStruct(state.shape, F32)],
        compiler_params=_params(("parallel",), 32 << 20),
        name="rwkv_sample",
    )(state, rowv(r), rowv(k), rowv(lw), rowv(a), rowv(v),
      rowp(k_k), rowp(k_a), rowp(r_k), rowp(gn_g), rowp(gn_b))
    return y.reshape(nb, heads * hd), new_state


def _gate_cast_body(y_ref, g_ref, o_ref):
    o_ref[...] = (y_ref[...] * g_ref[...]).astype(o_ref.dtype)


def _gate_cast(y, g):
    return pl.pallas_call(_gate_cast_body, out_shape=jax.ShapeDtypeStruct(y.shape, BF16), name="gate_cast")(y, g)


def kernel(x_prompt, x_sample, cache_a_k, cache_a_v, state_c_wkv, state_c_shift, p_prompt, p_sample, norm_g, final_norm_g, rel_bias, ab_w_in, ab_w_out, b_w_s, b_b_s, b_ln_g, b_ln_b, c_mu, c_w_r, c_w_k, c_w_v, c_w_g, c_w_o, c_w0, c_w1, c_w2, c_a0, c_a1, c_a2, c_k_k, c_k_a, c_r_k, c_gn_g, c_gn_b, ple_w_proj, ple_w_gate):
    batch, seq, d = x_prompt.shape
    nb = x_sample.shape[0]
    assert x_sample.shape[1] == 1 and seq % (QBLK * max(dil for _, dil in DILATION_PATTERNS)) == 0
    assert norm_g.shape[0] == 2, "layer pattern implemented for depth 2: one attention+gMLP layer, one RWKV-7 layer"
    a_heads = cache_a_k.shape[3]
    c_heads = state_c_wkv.shape[2]
    m = batch * seq
    hp = x_prompt.reshape(m, d)
    hs = x_sample.reshape(nb, d)
    pp = p_prompt.reshape(p_prompt.shape[0], m, -1)
    ps = p_sample.reshape(p_sample.shape[0], nb, -1)
    w_gate = ple_w_gate.astype(BF16)
    w_proj = ple_w_proj.astype(BF16)

    w_in = ab_w_in[0].astype(BF16)
    w_out = ab_w_out[0].astype(BF16)
    bias_prompt = _prompt_bias(rel_bias)
    bias_cache, bias_own = _sample_bias(rel_bias)

    q, k, v, ga, ob = _ab_in_proj(hp, norm_g[0], w_in, b_ln_g[0], b_ln_b[0], tm=256, w_s=b_w_s[0], b_s=b_b_s[0])
    oa = _attn_prompt(q, k, v, ga, bias_prompt, batch, seq)
    hp = _out_ple([oa, ob], hp, pp, 0, w_out, w_gate, w_proj, None, tm=256)
    a_k_p = k.reshape(1, batch, seq, a_heads, A_HEAD_DIM)
    a_v_p = v.reshape(1, batch, seq, a_heads, A_HEAD_DIM)

    qs, ks, vs, gas, ubs, vns, gbs = _ab_in_proj(hs, norm_g[0], w_in, b_ln_g[0], b_ln_b[0], tm=nb)
    oas = _attn_sample(qs, ks, vs, gas, cache_a_k[0], cache_a_v[0], bias_cache, bias_own)
    obs = _gmlp_sample(vns, ubs, gbs, b_w_s[0], b_b_s[0])
    hs = _out_ple([oas, obs], hs, ps, 0, w_out, w_gate, w_proj, None, tm=nb)
    a_k_s = ks.reshape(1, nb, 1, a_heads, A_HEAD_DIM)
    a_v_s = vs.reshape(1, nb, 1, a_heads, A_HEAD_DIM)
    b_v_s = vns.reshape(1, nb, 1, -1)

    mu = c_mu[0]
    w_o = c_w_o[0].astype(BF16)
    rk_flat = c_r_k[0].reshape(-1)
    lora = (c_w1[0], c_w2[0], c_w0[0], c_a1[0], c_a2[0], c_a0[0])

    def projections(h, prev, tm, seq_rows):
        r, kx, vx = _rwkv_proj(h, prev, norm_g[1], mu, [c_w_r[0], c_w_k[0], c_w_v[0]], (0, 2, 3), (None, None, None),
                               None, None, tm, seq_rows)
        g, lw, a = _rwkv_proj(h, prev, norm_g[1], mu, [c_w_g[0]], (5,), (_silu,), lora, (1, 4), tm, seq_rows)
        return r, kx, vx, g, lw, a

    r, kx, vx, g, lw, a = projections(hp, None, 256, seq)
    yg, s_p = _rwkv_prompt(r, kx, vx, lw, a, g, c_k_k[0], c_k_a[0], rk_flat, c_gn_g[0], c_gn_b[0], batch, seq)
    y_prompt = _out_ple([yg], hp, pp, 1, w_o, w_gate, w_proj, final_norm_g, tm=256)
    last_rows = hp.reshape(batch, seq, d)[:, seq - SUBLANES:, :].reshape(batch * SUBLANES, d)
    sh_p = _norm_rows(last_rows, norm_g[1]).reshape(batch, SUBLANES, d)[:, SUBLANES - 1]

    rs, kxs, vxs, gs, lws, a_s = projections(hs, state_c_shift[0], nb, 1)
    ys, s_s = _rwkv_sample(state_c_wkv[0], rs, kxs, vxs, lws, a_s, c_k_k[0], c_k_a[0], rk_flat, c_gn_g[0], c_gn_b[0])
    y_sample = _out_ple([_gate_cast(ys, gs)], hs, ps, 1, w_o, w_gate, w_proj, final_norm_g, tm=nb)
    sh_s = _norm_rows(hs, norm_g[1])

    return (y_prompt.reshape(batch, seq, d), y_sample.reshape(nb, 1, d), a_k_p, a_v_p, a_k_s, a_v_s, b_v_s,
            s_p[None], sh_p[None], s_s[None], sh_s[None])


def _norm_rows_body(x_ref, g_ref, o_ref):
    o_ref[...] = _rms_norm(x_ref[...], g_ref[...])


def _norm_rows(x, g):
    return pl.pallas_call(_norm_rows_body, out_shape=jax.ShapeDtypeStruct(x.shape, F32), name="norm_rows")(
        x, g.reshape(1, -1))
```

```python
import functools
import math

import jax
import jax.numpy as jnp
import numpy as np
from jax import lax
from jax.experimental import pallas as pl
from jax.experimental.pallas import tpu as pltpu

F32 = jnp.float32
BF16 = jnp.bfloat16

LANES = 128
SUBLANES = 8
VMEM_BUDGET_BYTES = 56 * 1024 * 1024

A_HEAD_DIM = 128
DILATION_PATTERNS = ((128, 1), (512, 4), (2048, 16))
QBLK = 128
ATTN_GROUP = 16
REL_BUCKETS = 32
REL_MAX_DIST = 2048
CHUNK = 128
C_HEAD_DIM = 64
TICK_EVERY = 12
RWKV_CHUNK = 64
RMS_EPS = 1e-6
LN_EPS = 1e-5
GN_EPS = 64e-5
NEG_INF = -1e30


def _params(semantics, vmem_bytes):
    return pltpu.CompilerParams(dimension_semantics=semantics, vmem_limit_bytes=int(vmem_bytes))


def _resident(shape, index_map):
    return pl.BlockSpec(shape, index_map, pipeline_mode=pl.Buffered(1))


def _bf16_terms(x, n):
    if x.dtype == BF16 or n == 1:
        return [x.astype(BF16)]
    terms, rest = [], x
    for _ in range(n):
        terms.append(rest.astype(BF16))
        rest = rest - terms[-1].astype(F32)
    return terms


def _dot_dims(a, b, dims, terms):
    a_terms = _bf16_terms(a, terms[0])
    b_terms = _bf16_terms(b, terms[1])
    out = None
    for i, at in enumerate(a_terms):
        for j, bt in enumerate(b_terms):
            if i + j < max(len(a_terms), len(b_terms)):
                part = lax.dot_general(at, bt, (dims, ((), ())), preferred_element_type=F32)
                out = part if out is None else out + part
    return out


def _dot(a, b, terms=(1, 1)):
    return _dot_dims(a, b, ((1,), (0,)), terms)


def _dot_nt(a, b, terms=(1, 1)):
    return _dot_dims(a, b, ((1,), (1,)), terms)


def _dot_tn(a, b, terms=(1, 1)):
    return _dot_dims(a, b, ((0,), (0,)), terms)


def _rms_norm(x, g):
    return x * lax.rsqrt(jnp.mean(x * x, axis=-1, keepdims=True) + RMS_EPS) * g


def _layer_norm(x, g, b):
    mu = jnp.mean(x, axis=-1, keepdims=True)
    var = jnp.mean(jnp.square(x - mu), axis=-1, keepdims=True)
    return (x - mu) * lax.rsqrt(var + LN_EPS) * g + b


def _sigmoid(x):
    return 0.5 * jnp.tanh(0.5 * x) + 0.5


def _silu(x):
    return x * _sigmoid(x)


def _ab_in_body(x_ref, g_ref, w_ref, lng_ref, lnb_ref, ws_ref, bs_ref, q_ref, k_ref, v_ref, ga_ref, ob_ref):
    xn = _rms_norm(x_ref[...], g_ref[...]).astype(BF16)
    width = q_ref.shape[1]
    column_group = lambda idx: _dot(xn, w_ref[:, idx * width:(idx + 1) * width])
    q_ref[...] = column_group(0)
    k_ref[...] = column_group(1)
    v_ref[...] = column_group(2)
    ga_ref[...] = _silu(column_group(3))
    ub = jax.nn.gelu(column_group(4))
    vn = _layer_norm(jax.nn.gelu(column_group(5)), lng_ref[...], lnb_ref[...])
    gb = _silu(column_group(6))
    ii = lax.broadcasted_iota(jnp.int32, (CHUNK, CHUNK), 0)
    jj = lax.broadcasted_iota(jnp.int32, (CHUNK, CHUNK), 1)
    vn = vn.astype(BF16)
    for grp in range(ws_ref.shape[0]):
        w = jnp.where(jj <= ii, ws_ref[grp], 0.0).astype(BF16)
        cols = slice(grp * LANES, (grp + 1) * LANES)
        for c in range(xn.shape[0] // CHUNK):
            rows = slice(c * CHUNK, (c + 1) * CHUNK)
            s = _dot(w, vn[rows, cols]) + bs_ref[:, cols]
            ob_ref[rows, cols] = (ub[rows, cols] * s * gb[rows, cols]).astype(ob_ref.dtype)


def _ab_in_cols_body(x_ref, g_ref, w_ref, lng_ref, lnb_ref, q_ref, k_ref, v_ref, ga_ref, ub_ref, vn_ref, gb_ref, xn_scr):
    j = pl.program_id(0)

    @pl.when(j == 0)
    def _():
        xn_scr[...] = _rms_norm(x_ref[...], g_ref[...]).astype(BF16)

    z = _dot(xn_scr[...], w_ref[...])
    epilogues = (
        (q_ref, lambda t: t),
        (k_ref, lambda t: t),
        (v_ref, lambda t: t),
        (ga_ref, _silu),
        (ub_ref, jax.nn.gelu),
        (vn_ref, lambda t: _layer_norm(jax.nn.gelu(t), lng_ref[...], lnb_ref[...])),
        (gb_ref, _silu),
    )
    for idx, (ref, fn) in enumerate(epilogues):
        @pl.when(j == idx)
        def _(ref=ref, fn=fn):
            ref[...] = fn(z)


def _ab_in_cols(x, norm_g, w_in_bf16, ln_g, ln_b):
    m, d = x.shape
    groups = 7
    width = w_in_bf16.shape[1] // groups
    fixed = lambda j: (0, 0)
    vmem = m * d * (2 * 4 + 2) + 2 * d * width * 2 + (2 * groups + 4) * m * width * 4 + (4 << 20)
    return pl.pallas_call(
        _ab_in_cols_body,
        grid=(groups,),
        in_specs=[pl.BlockSpec((m, d), fixed), pl.BlockSpec((1, d), fixed), pl.BlockSpec((d, width), lambda j: (0, j)),
                  pl.BlockSpec((1, width), fixed), pl.BlockSpec((1, width), fixed)],
        out_specs=[pl.BlockSpec((m, width), fixed)] * groups,
        out_shape=[jax.ShapeDtypeStruct((m, width), F32)] * groups,
        scratch_shapes=[pltpu.VMEM((m, d), BF16)],
        compiler_params=_params(("arbitrary",), vmem),
        name="ab_in_cols",
    )(x, norm_g.reshape(1, d), w_in_bf16, ln_g.reshape(1, width), ln_b.reshape(1, width))


def _ab_in_proj(x, norm_g, w_in_bf16, ln_g, ln_b, tm, w_s, b_s):
    m, d = x.shape
    width = w_in_bf16.shape[1] // 7
    out = jax.ShapeDtypeStruct((m, width), F32)
    row = lambda i: (i, 0)
    fixed = lambda i: (0, 0)
    groups = w_s.shape[0]
    assert tm % CHUNK == 0 and width == groups * LANES
    in_specs = [pl.BlockSpec((tm, d), row), pl.BlockSpec((1, d), fixed), _resident(w_in_bf16.shape, fixed),
                pl.BlockSpec((1, width), fixed), pl.BlockSpec((1, width), fixed),
                pl.BlockSpec((groups, CHUNK, CHUNK), lambda i: (0, 0, 0)), pl.BlockSpec((CHUNK, width), fixed)]
    args = [x, norm_g.reshape(1, d), w_in_bf16, ln_g.reshape(1, width), ln_b.reshape(1, width),
            w_s, jnp.repeat(b_s.T, width // groups, axis=1)]
    out_shape = [out] * 4 + [jax.ShapeDtypeStruct((m, width), BF16)]
    vmem = w_in_bf16.size * 2 + 2 * tm * d * 4 + len(out_shape) * 2 * tm * width * 4 + tm * d * 2 + 10 * tm * width * 4
    return pl.pallas_call(
        _ab_in_body,
        grid=(m // tm,),
        in_specs=in_specs,
        out_specs=[pl.BlockSpec((tm, width), row)] * len(out_shape),
        out_shape=out_shape,
        compiler_params=_params(("parallel",), vmem),
        name="ab_in_proj",
    )(*args)


def _t5_bucket(dist):
    n_exact = REL_BUCKETS // 2
    d = np.maximum(dist, 1).astype(np.float32)
    log_b = n_exact + (np.log(d / n_exact) / math.log(REL_MAX_DIST / n_exact) * (REL_BUCKETS - n_exact)).astype(np.int32)
    return np.where(dist < n_exact, dist, np.minimum(log_b, REL_BUCKETS - 1)).astype(np.int32)


def _bias_at(rel_bias, dist):
    one_hot = jnp.asarray(_t5_bucket(dist)[..., None] == np.arange(REL_BUCKETS), F32)
    return jnp.einsum("...k,kh->...h", one_hot, rel_bias.astype(F32), precision=lax.Precision.HIGHEST)


def _prompt_bias(rel_bias):
    i = np.arange(QBLK)[:, None]
    j = np.arange(2 * QBLK)[None, :]
    steps = QBLK + i - j
    tables = []
    for window, dil in DILATION_PATTERNS:
        band = (steps >= 0) & (steps <= window // dil)
        bias = jnp.moveaxis(_bias_at(rel_bias, np.clip(steps, 0, None) * dil), -1, 0)
        tables.append(jnp.where(band[None], bias, NEG_INF))
    return jnp.stack(tables)


def _sample_bias(rel_bias):
    back = QBLK - np.arange(QBLK)
    cache = jnp.stack([_bias_at(rel_bias, back * dil) for _, dil in DILATION_PATTERNS])
    own = _bias_at(rel_bias, np.zeros((), np.int32))
    return cache, own


def _attn_prompt_body(q_ref, k_ref, v_ref, gate_ref, bias_ref, o_ref, o_scr, lse_scr, *, seq):
    scale = A_HEAD_DIM ** -0.5

    def blocks(p, dil, starts, with_prev):
        ds = lambda st: pl.ds(st, QBLK, stride=dil) if dil > 1 else pl.ds(st, QBLK)
        rows = [ds(st) for st in starts]
        q = [q_ref[rw, :].astype(BF16) for rw in rows]
        k = [k_ref[rw, :] for rw in rows]
        v = [v_ref[rw, :] for rw in rows]
        if with_prev:
            prev = [ds(st - QBLK * dil) for st in starts]
            k = [jnp.concatenate([k_ref[pv, :], x], axis=0) for pv, x in zip(prev, k)]
            v = [jnp.concatenate([v_ref[pv, :], x], axis=0) for pv, x in zip(prev, v)]
            bias = bias_ref[p]
        else:
            bias = bias_ref[p, :, QBLK:]
        s = [_dot_nt(x, y) * scale + bias for x, y in zip(q, k)]
        m = [jnp.max(x, axis=-1, keepdims=True) for x in s]
        e = [jnp.exp(x - y) for x, y in zip(s, m)]
        l = [jnp.sum(x, axis=-1, keepdims=True) for x in e]
        o = [_dot(x, y) * (1.0 / z) for x, y, z in zip(e, v, l)]
        for rw, x, y, z in zip(rows, o, m, l):
            o_scr[p, rw, :] = x
            lse_scr[p, rw, :] = jnp.broadcast_to(y + jnp.log(z), (QBLK, A_HEAD_DIM))

    def widest_group(count):
        return max(g for g in range(1, ATTN_GROUP + 1) if count % g == 0)

    for p, (_, dil) in enumerate(DILATION_PATTERNS):
        nb = seq // (QBLK * dil)
        shift = dil.bit_length() - 1
        g_first = widest_group(dil)

        def first(i, carry, p=p, dil=dil, g=g_first):
            blocks(p, dil, [i * g + u for u in range(g)], False)
            return carry

        lax.fori_loop(0, dil // g_first, first, 0)
        if nb > 1:
            g_rest = widest_group(dil * (nb - 1))

            def rest(i, carry, p=p, dil=dil, shift=shift, g=g_rest):
                idx = [i * g + u for u in range(g)]
                blocks(p, dil, [((x >> shift) + 1) * (QBLK * dil) + (x & (dil - 1)) for x in idx], True)
                return carry

            lax.fori_loop(0, dil * (nb - 1) // g_rest, rest, 0)

    lse = [lse_scr[p] for p in range(len(DILATION_PATTERNS))]
    m = functools.reduce(jnp.maximum, lse)
    e = [jnp.exp(t - m) for t in lse]
    den = functools.reduce(jnp.add, e)
    inv_den = 1.0 / den
    mix = functools.reduce(jnp.add, [(e[p] * inv_den) * o_scr[p] for p in range(len(e))])
    o_ref[...] = (mix * gate_ref[...]).astype(o_ref.dtype)


def _attn_prompt(q, k, v, gate, bias, batch, seq):
    width = q.shape[1]
    heads = width // A_HEAD_DIM
    n_pat = len(DILATION_PATTERNS)
    blk = pl.BlockSpec((None, seq, A_HEAD_DIM), lambda b, h: (b, 0, h))
    r3 = lambda t: t.reshape(batch, seq, width)
    vmem = 4 * 2 * seq * A_HEAD_DIM * 4 + 2 * seq * A_HEAD_DIM * 2 + 2 * n_pat * seq * A_HEAD_DIM * 4 \
        + 2 * n_pat * QBLK * 2 * QBLK * 4 + (8 << 20)
    out = pl.pallas_call(
        functools.partial(_attn_prompt_body, seq=seq),
        grid=(batch, heads),
        in_specs=[blk, blk, blk, blk,
                  pl.BlockSpec((n_pat, None, QBLK, 2 * QBLK), lambda b, h: (0, h, 0, 0))],
        out_specs=blk,
        out_shape=jax.ShapeDtypeStruct((batch, seq, width), BF16),
        scratch_shapes=[pltpu.VMEM((n_pat, seq, A_HEAD_DIM), F32), pltpu.VMEM((n_pat, seq, A_HEAD_DIM), F32)],
        compiler_params=_params(("parallel", "parallel"), vmem),
        name="attn_prompt",
    )(r3(q), r3(k), r3(v), r3(gate), bias)
    return out.reshape(batch * seq, width)


def _attn_sample_body(q_ref, kn_ref, vn_ref, gate_ref, bc_ref, bo_ref, *rest):
    n_pat = len(DILATION_PATTERNS)
    kc_refs, vc_refs, o_ref = rest[:n_pat], rest[n_pat:2 * n_pat], rest[2 * n_pat]
    scale = A_HEAD_DIM ** -0.5
    rnd = lambda t: t.astype(BF16).astype(F32)
    q = rnd(q_ref[...])
    kn = rnd(kn_ref[...])
    vn = rnd(vn_ref[...])
    s_new = jnp.sum(q * kn, axis=-1, keepdims=True) * scale + bo_ref[...]
    outs, lses = [], []
    for p in range(n_pat):
        kc = rnd(kc_refs[p][...])
        vc = rnd(vc_refs[p][...])
        s = jnp.sum(q[None] * kc, axis=-1, keepdims=True) * scale + bc_ref[p]
        m = jnp.maximum(jnp.max(s, axis=0), s_new)
        e = jnp.exp(s - m[None])
        e_new = jnp.exp(s_new - m)
        l = jnp.sum(e, axis=0) + e_new
        outs.append((jnp.sum(rnd(e) * vc, axis=0) + rnd(e_new) * vn) / l)
        lses.append(m + jnp.log(l))
    m = functools.reduce(jnp.maximum, lses)
    e = [jnp.exp(t - m) for t in lses]
    den = functools.reduce(jnp.add, e)
    mix = functools.reduce(jnp.add, [(e[p] / den) * outs[p] for p in range(n_pat)])
    o_ref[...] = (mix * gate_ref[...]).astype(o_ref.dtype)


def _attn_sample(q, k_new, v_new, gate, cache_k, cache_v, bias_cache, bias_own):
    nb, past, heads, hd = cache_k.shape
    n_pat = len(DILATION_PATTERNS)
    row = pl.BlockSpec((None, heads, hd), lambda b: (b, 0, 0))
    r3 = lambda t: t.reshape(nb, heads, hd)
    cache_specs, cache_args = [], []
    for cache in (cache_k, cache_v):
        for window, dil in DILATION_PATTERNS:
            last = past // (QBLK * dil) - 1
            cache_specs.append(pl.BlockSpec((None, QBLK, None, heads, hd), lambda b, last=last: (b, last, 0, 0, 0)))
            cache_args.append(cache.reshape(nb, past // dil, dil, heads, hd))
    vmem = 2 * 2 * n_pat * QBLK * heads * hd * 4 + 8 * QBLK * heads * hd * 4 + (8 << 20)
    out = pl.pallas_call(
        _attn_sample_body,
        grid=(nb,),
        in_specs=[row, row, row, row,
                  pl.BlockSpec((n_pat, QBLK, heads, 1), lambda b: (0, 0, 0, 0)),
                  pl.BlockSpec((heads, 1), lambda b: (0, 0))] + cache_specs,
        out_specs=row,
        out_shape=jax.ShapeDtypeStruct((nb, heads, hd), BF16),
        compiler_params=_params(("parallel",), vmem),
        name="attn_sample",
    )(r3(q), r3(k_new), r3(v_new), r3(gate), bias_cache[..., None], bias_own.reshape(heads, 1), *cache_args)
    return out.reshape(nb, heads * hd)


def _gmlp_sample_body(vn_ref, ub_ref, gb_ref, w0_ref, b0_ref, o_ref):
    s = w0_ref[...].astype(BF16).astype(F32) * vn_ref[...].astype(BF16).astype(F32) + b0_ref[...]
    o_ref[...] = (ub_ref[...] * s * gb_ref[...]).astype(o_ref.dtype)


def _gmlp_sample(vn, ub, gb, w_s, b_s):
    m, width = vn.shape
    groups = w_s.shape[0]
    w0 = jnp.repeat(w_s[:, 0, 0], width // groups).reshape(1, width)
    b0 = jnp.repeat(b_s[:, 0], width // groups).reshape(1, width)
    return pl.pallas_call(
        _gmlp_sample_body,
        out_shape=jax.ShapeDtypeStruct((m, width), BF16),
        name="gmlp_sample",
    )(vn, ub, gb, w0, b0)


def _out_ple_body(*refs, n_a, final_norm):
    a_refs = refs[:n_a]
    x_ref, p_ref, wo_ref, wg_ref, wp_ref = refs[n_a:n_a + 5]
    rest = refs[n_a + 5:]
    h = x_ref[...]
    off = 0
    for a_ref in a_refs:
        ka = a_ref.shape[1]
        h = h + _dot(a_ref[...], wo_ref[off:off + ka, :])
        off += ka
    gate = _sigmoid(_dot(h.astype(BF16), wg_ref[...]))
    h = h + gate * _dot(p_ref[...].astype(BF16), wp_ref[...])
    if final_norm:
        fg_ref, o_ref = rest
        o_ref[...] = _rms_norm(h, fg_ref[...])
    else:
        (o_ref,) = rest
        o_ref[...] = h


def _out_ple(a_list, x, p_layers, layer, w_out_bf16, w_gate_layers, w_proj_layers, final_g, tm):
    m, d = x.shape
    pd = p_layers.shape[2]
    row = lambda i: (i, 0)
    fixed = lambda i: (0, 0)
    of_layer = lambda i: (layer, 0, 0)
    in_specs = [pl.BlockSpec((tm, a.shape[1]), row) for a in a_list]
    in_specs += [pl.BlockSpec((tm, d), row), pl.BlockSpec((None, tm, pd), lambda i: (layer, i, 0)),
                 _resident(w_out_bf16.shape, fixed), _resident((None, d, d), of_layer), _resident((None, pd, d), of_layer)]
    args = list(a_list) + [x, p_layers, w_out_bf16, w_gate_layers, w_proj_layers]
    if final_g is not None:
        in_specs.append(pl.BlockSpec((1, d), fixed))
        args.append(final_g.reshape(1, d))
    vmem = (w_out_bf16.size + d * d + pd * d) * 2 + 2 * tm * (2 * d * 4 + pd * 4 + w_out_bf16.shape[0] * 2) \
        + 4 * tm * d * 4 + (6 << 20)
    return pl.pallas_call(
        functools.partial(_out_ple_body, n_a=len(a_list), final_norm=final_g is not None),
        grid=(m // tm,),
        in_specs=in_specs,
        out_specs=pl.BlockSpec((tm, d), row),
        out_shape=jax.ShapeDtypeStruct((m, d), F32),
        compiler_params=_params(("parallel",), vmem),
        name="out_ple",
    )(*args)


def _shifted_inputs(h_ref, prev_ref, g_ref, *, seq_rows, prev_given):
    xn = _rms_norm(h_ref[...], g_ref[...])
    if prev_given:
        return xn, prev_ref[...]
    tm = xn.shape[0]
    before = _rms_norm(prev_ref[...], g_ref[...])[SUBLANES - 1:SUBLANES, :]
    starts_sequence = (pl.program_id(0) * tm) % seq_rows == 0
    before = jnp.where(starts_sequence, 0.0, before)
    first = lax.broadcasted_iota(jnp.int32, xn.shape, 0) == 0
    return xn, jnp.where(first, before, pltpu.roll(xn, 1, 0))


def _rwkv_proj_body(h_ref, prev_ref, g_ref, mu_ref, *rest, seq_rows, prev_given, mixes, acts, lora_mixes):
    xn, x_prev = _shifted_inputs(h_ref, prev_ref, g_ref, seq_rows=seq_rows, prev_given=prev_given)
    xx = x_prev - xn
    mixed = lambda idx: (xn + xx * mu_ref[idx:idx + 1, :]).astype(BF16)
    w_refs, rest = rest[:len(mixes)], rest[len(mixes):]
    if lora_mixes is not None:
        w1_ref, w2_ref, w0_ref, a1_ref, a2_ref, a0_ref = rest[:6]
        rest = rest[6:]
    for i, (mix, act) in enumerate(zip(mixes, acts)):
        z = _dot(mixed(mix), w_refs[i][...])
        rest[i][...] = act(z) if act is not None else z
    if lora_mixes is not None:
        lw_ref, a_ref = rest[len(mixes):]
        zw = w0_ref[...] + _dot(jnp.tanh(_dot(mixed(lora_mixes[0]), w1_ref[...])), w2_ref[...])
        lw_ref[...] = -math.exp(-0.5) * _sigmoid(zw)
        za = a0_ref[...] + _dot(_dot(mixed(lora_mixes[1]), a1_ref[...]), a2_ref[...])
        a_ref[...] = _sigmoid(za)


def _pad_lora(w_down, w_up):
    rank = w_down.shape[1]
    pad = (-rank) % LANES
    return (jnp.pad(w_down, ((0, 0), (0, pad))).astype(BF16), jnp.pad(w_up, ((0, pad), (0, 0))).astype(BF16))


def _rwkv_proj(h, prev, norm_g, mu, weights, mixes, acts, lora, lora_mixes, tm, seq_rows):
    m, d = h.shape
    prev_given = prev is not None
    row = lambda i: (i, 0)
    fixed = lambda i: (0, 0)
    if prev_given:
        specs, args = [pl.BlockSpec((tm, d), row), pl.BlockSpec((tm, d), row)], [h, prev]
    else:
        per = tm // SUBLANES
        specs = [pl.BlockSpec((tm, d), row), pl.BlockSpec((SUBLANES, d), lambda i: (jnp.maximum(i * per - 1, 0), 0))]
        args = [h, h]
    specs += [pl.BlockSpec((1, d), fixed), pl.BlockSpec(mu.shape, fixed)] + [_resident((d, d), fixed)] * len(weights)
    args += [norm_g.reshape(1, d), mu] + [w.astype(BF16) for w in weights]
    n_out = len(weights)
    lora_bytes = 0
    if lora is not None:
        w1, w2, w0, a1, a2, a0 = lora
        w1p, w2p = _pad_lora(w1, w2)
        a1p, a2p = _pad_lora(a1, a2)
        rank = w1p.shape[1]
        vec, down, up = pl.BlockSpec((1, d), fixed), pl.BlockSpec((d, rank), fixed), pl.BlockSpec((rank, d), fixed)
        specs += [down, up, vec, down, up, vec]
        args += [w1p, w2p, w0.reshape(1, d), a1p, a2p, a0.reshape(1, d)]
        n_out += 2
        lora_bytes = 2 * 4 * d * rank * 2
    out = jax.ShapeDtypeStruct((m, d), F32)
    vmem = len(weights) * d * d * 2 + lora_bytes + (2 + 2 * n_out + 8) * tm * d * 4 + (2 << 20)
    return pl.pallas_call(
        functools.partial(_rwkv_proj_body, seq_rows=seq_rows, prev_given=prev_given, mixes=mixes, acts=acts,
                          lora_mixes=lora_mixes if lora is not None else None),
        grid=(m // tm,),
        in_specs=specs,
        out_specs=[pl.BlockSpec((tm, d), row)] * n_out,
        out_shape=[out] * n_out,
        compiler_params=_params(("parallel",), vmem),
        name="rwkv_proj",
    )(*args)


def _rwkv_prompt_body(r_ref, k_ref, v_ref, lw_ref, a_ref, g_ref, kk_ref, ka_ref, rk_ref, gng_ref, gnb_ref,
                      o_ref, st_ref, q2_scr, y0_scr, m_scr, h0_scr, bv_scr, gate_scr, hs_scr, *, seq):
    c_len = RWKV_CHUNK
    n_chunks = seq // c_len
    hd = C_HEAD_DIM
    assert c_len == hd and LANES % hd == 0, "time x time and key x value blocks share one lane tiling"
    lane = lax.broadcasted_iota(jnp.int32, (1, LANES), 1)
    head_masks = (lane < hd, lane >= hd)
    ti = lax.broadcasted_iota(jnp.int32, (c_len, c_len), 0)
    si = lax.broadcasted_iota(jnp.int32, (c_len, c_len), 1)
    tri = (ti >= si).astype(F32)
    tp = lax.broadcasted_iota(jnp.int32, (c_len, LANES), 0)
    sp = lax.broadcasted_iota(jnp.int32, (c_len, LANES), 1) % hd
    incl = tp >= sp
    strict = tp > sp
    eye_c = (tp == sp).astype(F32)
    bi = lax.broadcasted_iota(jnp.int32, (LANES, LANES), 0)
    bj = lax.broadcasted_iota(jnp.int32, (LANES, LANES), 1)
    same_head = (bi // hd) == (bj // hd)
    head_ones = same_head.astype(BF16)

    def head_sum(x):
        return _dot(x, head_ones)

    def stacked(z):
        masks = [jnp.concatenate([mk] * (z.shape[1] // LANES), axis=1) for mk in head_masks]
        return jnp.concatenate([jnp.where(mk, z, 0.0) for mk in masks], axis=0)

    def block_diag(x):
        return jnp.where(same_head, jnp.concatenate([x] * len(head_masks), axis=0), 0.0)

    def diag_blocks(x):
        x = jnp.where(same_head, x, 0.0)
        return x[:hd] + x[hd:]

    step = pl.program_id(0)
    cur = lax.rem(step, 2)
    prev = 1 - cur

    @pl.when(step == 0)
    def _():
        for scr in (q2_scr, y0_scr, m_scr, h0_scr, bv_scr, gate_scr):
            scr[1] = jnp.zeros(scr.shape[1:], scr.dtype)

    chunks = list(range(n_chunks))
    rows = [slice(c * c_len, (c + 1) * c_len) for c in chunks]
    state = [jnp.zeros((hd, LANES), F32)]
    carried = [0]

    def tick():
        c = carried[0]
        if c < n_chunks:
            hs_scr[c] = state[0]
            state[0] = _dot(m_scr[prev, c], stacked(state[0]), (2, 2)) + h0_scr[prev, c]
            carried[0] = c + 1

    def stage(fn, *lists):
        out = []
        for i, args in enumerate(zip(*lists)):
            out.append(fn(*args))
            if i % TICK_EVERY == TICK_EVERY - 1:
                tick()
        return out

    r = [r_ref[rw, :] for rw in rows]
    k = [k_ref[rw, :] for rw in rows]
    v = [v_ref[rw, :] for rw in rows]
    a = [a_ref[rw, :] for rw in rows]
    lw = [lw_ref[rw, :] for rw in rows]
    k2 = [x * (1.0 + (y - 1.0) * ka_ref[...]) for x, y in zip(k, a)]
    kk = [x * kk_ref[...] for x in k]
    norm = stage(lambda x: head_sum(x * x), kk)
    cum = stage(lambda x: _dot(tri, x, (1, 2)), lw)
    bonus = stage(lambda x, y: head_sum(x * y * rk_ref[...]), r, k2)
    kk = [x * lax.rsqrt(jnp.maximum(n, 1e-24)) for x, n in zip(kk, norm)]
    last = [x[c_len - 1:c_len, :] for x in cum]
    b = [x * y for x, y in zip(kk, a)]
    ekk = [x * jnp.exp(cm - w) for x, cm, w in zip(kk, cum, lw)]
    tick()
    er = [x * jnp.exp(cm) for x, cm in zip(r, cum)]
    inv = [jnp.exp(-cm) for cm in cum]
    eb = [x * y for x, y in zip(b, inv)]
    ek = [x * y for x, y in zip(k2, inv)]
    tick()
    w_end = [jnp.exp(ls) for ls in last]
    eb_end = [x * y for x, y in zip(eb, w_end)]
    ek_end = [x * y for x, y in zip(ek, w_end)]
    lhs = [jnp.concatenate([x, y], axis=0) for x, y in zip(ekk, er)]
    g = stage(lambda x, y, z: _dot_nt(x, jnp.concatenate([stacked(y), stacked(z)], axis=0)), lhs, eb, ek)
    l_b = [jnp.where(strict, x[:c_len, :LANES], 0.0) for x in g]
    a_b = [jnp.where(incl, x[c_len:, :LANES], 0.0) for x in g]
    l_k = [jnp.where(strict, x[:c_len, LANES:], 0.0) for x in g]
    a_k = [jnp.where(incl, x[c_len:, LANES:], 0.0) for x in g]
    lakv = stage(lambda x, y, z: _dot(jnp.concatenate([x, y], axis=0), stacked(z)), l_k, a_k, v)
    lkv = [x[:c_len] for x in lakv]
    akv = [x[c_len:] for x in lakv]
    levels = int(math.log2(c_len))
    power = [-x for x in l_b]
    inv_t = [eye_c + x for x in power]
    power = stage(lambda x: _dot(x, block_diag(x)), power)
    for level in range(1, levels):
        if level < levels - 1:
            both = stage(lambda x, y: _dot(jnp.concatenate([x, y], axis=0), block_diag(x)), power, inv_t)
            power = [x[:c_len] for x in both]
            inv_t = [x + y[c_len:] for x, y in zip(inv_t, both)]
        else:
            inv_t = stage(lambda x, y: y + _dot(y, block_diag(x)), power, inv_t)
    tz = stage(lambda x, y, z: _dot(x, stacked(jnp.concatenate([y, z], axis=1))), inv_t, ekk, lkv)
    qu = [jnp.concatenate([x[:, :LANES], -x[:, LANES:]], axis=1) for x in tz]
    ab = stage(lambda x, y: _dot(x, stacked(y)), a_b, qu)
    ends = [jnp.concatenate([x, y], axis=0) for x, y in zip(eb_end, ek_end)]
    tails = [jnp.concatenate([x, jnp.concatenate([jnp.zeros_like(y), y], axis=1)], axis=0) for x, y in zip(qu, v)]
    bq = stage(_dot_tn, ends, tails)
    for c in chunks:
        q2_scr[cur, c] = er[c] - ab[c][:, :LANES]
        y0_scr[cur, c] = ab[c][:, LANES:] + akv[c]
        m_scr[cur, c] = eye_c * jnp.exp(last[c]) - diag_blocks(bq[c][:, :LANES])
        h0_scr[cur, c] = diag_blocks(bq[c][:, LANES:])
        bv_scr[cur, rows[c], :] = bonus[c] * v[c]
        gate_scr[cur, rows[c], :] = g_ref[rows[c], :]
    def emit(which):
        y = [_dot(q2_scr[prev, c], stacked(hs_scr[c])) + y0_scr[prev, c] for c in which]
        tick()
        mean = [head_sum(x) * (1.0 / hd) for x in y]
        tick()
        dev = [x - mu for x, mu in zip(y, mean)]
        var = [head_sum(x * x) * (1.0 / hd) for x in dev]
        tick()
        for c, dv, vr in zip(which, dev, var):
            yn = dv * lax.rsqrt(vr + GN_EPS) * gng_ref[...] + gnb_ref[...]
            o_ref[rows[c], :] = ((yn + bv_scr[prev, rows[c], :]) * gate_scr[prev, rows[c], :]).astype(o_ref.dtype)

    ready = carried[0]
    emit(chunks[:ready])
    while carried[0] < n_chunks:
        tick()
    st_ref[...] = _dot_tn(state[0], (ti == si).astype(F32), (3, 1))
    emit(chunks[ready:])


def _rwkv_prompt(r, k, v, lw, a, g, k_k, k_a, r_k, gn_g, gn_b, batch, seq):
    d = r.shape[1]
    pairs = d // LANES
    n_seq = batch * pairs
    n_chunks = seq // RWKV_CHUNK
    hd = C_HEAD_DIM
    this = lambda s: jnp.minimum(s, n_seq - 1)
    last = lambda s: jnp.maximum(s - 1, 0)
    blk = pl.BlockSpec((None, seq, LANES), lambda s: (this(s) // pairs, 0, this(s) % pairs))
    vec = pl.BlockSpec((1, LANES), lambda s: (0, this(s) % pairs))
    vec_last = pl.BlockSpec((1, LANES), lambda s: (0, last(s) % pairs))
    r3 = lambda t: t.reshape(batch, seq, d)
    v2 = lambda t: t.reshape(1, d)
    chunk_maps = pltpu.VMEM((2, n_chunks, RWKV_CHUNK, LANES), F32)
    rows_saved = pltpu.VMEM((2, seq, LANES), F32)
    vmem = 2 * 6 * seq * LANES * 4 + 2 * seq * LANES * 2 + (4 * 2 + 2 * 2 + 1) * seq * LANES * 4 + (24 << 20)
    out, state = pl.pallas_call(
        functools.partial(_rwkv_prompt_body, seq=seq),
        grid=(n_seq + 1,),
        in_specs=[blk] * 6 + [vec] * 3 + [vec_last] * 2,
        out_specs=[pl.BlockSpec((None, seq, LANES), lambda s: (last(s) // pairs, 0, last(s) % pairs)),
                   pl.BlockSpec((None, None, LANES, hd), lambda s: (last(s) // pairs, last(s) % pairs, 0, 0))],
        out_shape=[jax.ShapeDtypeStruct((batch, seq, d), BF16),
                   jax.ShapeDtypeStruct((batch, pairs, LANES, hd), F32)],
        scratch_shapes=[chunk_maps] * 4 + [rows_saved] * 2 + [pltpu.VMEM((n_chunks, RWKV_CHUNK, LANES), F32)],
        compiler_params=_params(("arbitrary",), vmem),
        name="rwkv_prompt",
    )(r3(r), r3(k), r3(v), r3(lw), r3(a), r3(g), v2(k_k), v2(k_a), v2(r_k), v2(gn_g), v2(gn_b))
    return out.reshape(batch * seq, d), state.reshape(batch, d // hd, hd, hd)


def _rwkv_sample_body(s_ref, r_ref, k_ref, lw_ref, a_ref, v_ref, kk_ref, ka_ref, rk_ref, gng_ref, gnb_ref,
                      y_ref, so_ref):
    r, k, a, v = r_ref[...], k_ref[...], a_ref[...], v_ref[...]
    s = s_ref[...]

    def flipped(x):
        return jnp.swapaxes(jnp.broadcast_to(x, s.shape), 1, 2)

    kk = k * kk_ref[...]
    kk = kk / jnp.maximum(jnp.sqrt(jnp.sum(kk * kk, axis=-1, keepdims=True)), 1e-12)
    k2 = k * (1.0 + (a - 1.0) * ka_ref[...])
    sa = -jnp.sum(s * kk, axis=-1, keepdims=True)
    s = s * jnp.exp(lw_ref[...]) + sa * (kk * a) + flipped(v) * k2
    so_ref[...] = s
    y = flipped(jnp.sum(s * r, axis=-1, keepdims=True))[:, 0:1, :]
    mean = jnp.mean(y, axis=-1, keepdims=True)
    var = jnp.mean(jnp.square(y - mean), axis=-1, keepdims=True)
    y = (y - mean) * lax.rsqrt(var + GN_EPS) * gng_ref[...] + gnb_ref[...]
    bonus = jnp.sum(r * k2 * rk_ref[...], axis=-1, keepdims=True)
    y_ref[...] = y + bonus * v


def _rwkv_sample(state, r, k, v, lw, a, k_k, k_a, r_k, gn_g, gn_b):
    nb, heads, hd, _ = state.shape
    rowv = lambda t: t.reshape(nb, heads, 1, hd)
    rowp = lambda t: t.reshape(heads, 1, hd)
    st = pl.BlockSpec((None, heads, hd, hd), lambda b: (b, 0, 0, 0))
    rv = pl.BlockSpec((None, heads, 1, hd), lambda b: (b, 0, 0, 0))
    rp = pl.BlockSpec((heads, 1, hd), lambda b: (0, 0, 0))
    y, new_state = pl.pallas_call(
        _rwkv_sample_body,
        grid=(nb,),
        in_specs=[st, rv, rv, rv, rv, rv, rp, rp, rp, rp, rp],
        out_specs=[rv, st],
        out_shape=[jax.ShapeDtypeStruct((nb, heads, 1, hd), F32), jax.ShapeDtypeStruct(state.shape, F32)],
        compiler_params=_params(("parallel",), 32 << 20),
        name="rwkv_sample",
    )(state, rowv(r), rowv(k), rowv(lw), rowv(a), rowv(v),
      rowp(k_k), rowp(k_a), rowp(r_k), rowp(gn_g), rowp(gn_b))
    return y.reshape(nb, heads * hd), new_state


def _gate_cast_body(y_ref, g_ref, o_ref):
    o_ref[...] = (y_ref[...] * g_ref[...]).astype(o_ref.dtype)


def _gate_cast(y, g):
    return pl.pallas_call(_gate_cast_body, out_shape=jax.ShapeDtypeStruct(y.shape, BF16), name="gate_cast")(y, g)


def kernel(x_prompt, x_sample, cache_a_k, cache_a_v, state_c_wkv, state_c_shift, p_prompt, p_sample, norm_g, final_norm_g, rel_bias, ab_w_in, ab_w_out, b_w_s, b_b_s, b_ln_g, b_ln_b, c_mu, c_w_r, c_w_k, c_w_v, c_w_g, c_w_o, c_w0, c_w1, c_w2, c_a0, c_a1, c_a2, c_k_k, c_k_a, c_r_k, c_gn_g, c_gn_b, ple_w_proj, ple_w_gate):
    batch, seq, d = x_prompt.shape
    nb = x_sample.shape[0]
    assert x_sample.shape[1] == 1 and seq % (QBLK * max(dil for _, dil in DILATION_PATTERNS)) == 0
    assert norm_g.shape[0] == 2, "layer pattern implemented for depth 2: one attention+gMLP layer, one RWKV-7 layer"
    a_heads = cache_a_k.shape[3]
    c_heads = state_c_wkv.shape[2]
    m = batch * seq
    hp = x_prompt.reshape(m, d)
    hs = x_sample.reshape(nb, d)
    pp = p_prompt.reshape(p_prompt.shape[0], m, -1)
    ps = p_sample.reshape(p_sample.shape[0], nb, -1)
    w_gate = ple_w_gate.astype(BF16)
    w_proj = ple_w_proj.astype(BF16)

    w_in = ab_w_in[0].astype(BF16)
    w_out = ab_w_out[0].astype(BF16)
    bias_prompt = _prompt_bias(rel_bias)
    bias_cache, bias_own = _sample_bias(rel_bias)

    q, k, v, ga, ob = _ab_in_proj(hp, norm_g[0], w_in, b_ln_g[0], b_ln_b[0], tm=256, w_s=b_w_s[0], b_s=b_b_s[0])
    oa = _attn_prompt(q, k, v, ga, bias_prompt, batch, seq)
    hp = _out_ple([oa, ob], hp, pp, 0, w_out, w_gate, w_proj, None, tm=256)
    a_k_p = k.reshape(1, batch, seq, a_heads, A_HEAD_DIM)
    a_v_p = v.reshape(1, batch, seq, a_heads, A_HEAD_DIM)

    qs, ks, vs, gas, ubs, vns, gbs = _ab_in_cols(hs, norm_g[0], w_in, b_ln_g[0], b_ln_b[0])
    oas = _attn_sample(qs, ks, vs, gas, cache_a_k[0], cache_a_v[0], bias_cache, bias_own)
    obs = _gmlp_sample(vns, ubs, gbs, b_w_s[0], b_b_s[0])
    hs = _out_ple([oas, obs], hs, ps, 0, w_out, w_gate, w_proj, None, tm=nb)
    a_k_s = ks.reshape(1, nb, 1, a_heads, A_HEAD_DIM)
    a_v_s = vs.reshape(1, nb, 1, a_heads, A_HEAD_DIM)
    b_v_s = vns.reshape(1, nb, 1, -1)

    mu = c_mu[0]
    w_o = c_w_o[0].astype(BF16)
    rk_flat = c_r_k[0].reshape(-1)
    lora = (c_w1[0], c_w2[0], c_w0[0], c_a1[0], c_a2[0], c_a0[0])

    def projections(h, prev, tm, seq_rows):
        r, kx, vx = _rwkv_proj(h, prev, norm_g[1], mu, [c_w_r[0], c_w_k[0], c_w_v[0]], (0, 2, 3), (None, None, None),
                               None, None, tm, seq_rows)
        g, lw, a = _rwkv_proj(h, prev, norm_g[1], mu, [c_w_g[0]], (5,), (_silu,), lora, (1, 4), tm, seq_rows)
        return r, kx, vx, g, lw, a

    r, kx, vx, g, lw, a = projections(hp, None, 256, seq)
    yg, s_p = _rwkv_prompt(r, kx, vx, lw, a, g, c_k_k[0], c_k_a[0], rk_flat, c_gn_g[0], c_gn_b[0], batch, seq)
    y_prompt = _out_ple([yg], hp, pp, 1, w_o, w_gate, w_proj, final_norm_g, tm=256)
    last_rows = hp.reshape(batch, seq, d)[:, seq - SUBLANES:, :].reshape(batch * SUBLANES, d)
    sh_p = _norm_rows(last_rows, norm_g[1]).reshape(batch, SUBLANES, d)[:, SUBLANES - 1]

    rs, kxs, vxs, gs, lws, a_s = projections(hs, state_c_shift[0], nb, 1)
    ys, s_s = _rwkv_sample(state_c_wkv[0], rs, kxs, vxs, lws, a_s, c_k_k[0], c_k_a[0], rk_flat, c_gn_g[0], c_gn_b[0])
    y_sample = _out_ple([_gate_cast(ys, gs)], hs, ps, 1, w_o, w_gate, w_proj, final_norm_g, tm=nb)
    sh_s = _norm_rows(hs, norm_g[1])

    return (y_prompt.reshape(batch, seq, d), y_sample.reshape(nb, 1, d), a_k_p, a_v_p, a_k_s, a_v_s, b_v_s,
            s_p[None], sh_p[None], s_s[None], sh_s[None])


def _norm_rows_body(x_ref, g_ref, o_ref):
    o_ref[...] = _rms_norm(x_ref[...], g_ref[...])


def _norm_rows(x, g):
    return pl.pallas_call(_norm_rows_body, out_shape=jax.ShapeDtypeStruct(x.shape, F32), name="norm_rows")(
        x, g.reshape(1, -1))
```
